```python
import math
import jax, jax.numpy as jnp
from jax import lax
import numpy as np

D_MODEL = 4096
BATCH = 4
SEQ = 4096
DEPTH = 2

N_A_LAYERS = DEPTH // 2
N_B_LAYERS = DEPTH - N_A_LAYERS

RET_QK_DIM = 256
RET_HEADS = D_MODEL // RET_QK_DIM
RET_V_HEAD_DIM = 2 * RET_QK_DIM
RET_QK = RET_HEADS * RET_QK_DIM
RET_V = RET_HEADS * RET_V_HEAD_DIM
RET_IN = 2 * RET_QK + 2 * RET_V
RET_CHUNK = 128
ROPE_BASE = 10000.0

SB_HEAD_DIM = 128
SB_HEADS = D_MODEL // SB_HEAD_DIM
SB_BLOCK = 128

PEER_HEADS = 8
PEER_N_KEYS = 128
PEER_EXPERTS = PEER_N_KEYS * PEER_N_KEYS
PEER_TOPK = 16
PEER_QUERY_DIM = 256
PEER_HALF = PEER_QUERY_DIM // 2
PEER_TOKEN_BLOCK = 128

NORM_EPS = 1e-6

kernel_name = "yoco_retention_stickbreak_peer"


def rms_norm(x, g):
    xf = x.astype(jnp.float32)
    y = xf * lax.rsqrt(jnp.mean(xf * xf, axis=-1, keepdims=True) + NORM_EPS)
    return (y * g.astype(jnp.float32)).astype(x.dtype)


def rotary(t):
    S, d = t.shape[1], t.shape[-1]
    inv = 1.0 / (ROPE_BASE ** (jnp.arange(0, d, 2, dtype=jnp.float32) / d))
    ang = jnp.arange(S, dtype=jnp.float32)[:, None] * inv[None, :]
    cos = jnp.cos(ang)[None, :, None, :]
    sin = jnp.sin(ang)[None, :, None, :]
    tf = t.astype(jnp.float32)
    t1, t2 = tf[..., : d // 2], tf[..., d // 2:]
    return jnp.concatenate([t1 * cos - t2 * sin, t1 * sin + t2 * cos], axis=-1)


def retention_layer(xn, w_in, head_norm_g, w_out):
    B, S, _ = xn.shape
    proj = xn @ w_in
    q = proj[..., :RET_QK].reshape(B, S, RET_HEADS, RET_QK_DIM)
    k = proj[..., RET_QK:2 * RET_QK].reshape(B, S, RET_HEADS, RET_QK_DIM)
    v = proj[..., 2 * RET_QK:2 * RET_QK + RET_V].reshape(B, S, RET_HEADS, RET_V_HEAD_DIM)
    gate = proj[..., 2 * RET_QK + RET_V:]
    q = rotary(q)
    k = rotary(k) * (RET_QK_DIM ** -0.5)
    v = v.astype(jnp.float32)

    C = RET_CHUNK
    n = S // C
    def to_chunks(t):
        return t.reshape(B, n, C, RET_HEADS, t.shape[-1]).transpose(1, 0, 3, 2, 4)
    qc, kc, vc = to_chunks(q), to_chunks(k), to_chunks(v)

    log_g = jnp.log1p(-jnp.exp2(-5.0 - jnp.arange(RET_HEADS, dtype=jnp.float32)))
    i = jnp.arange(C, dtype=jnp.float32)
    diff = i[:, None] - i[None, :]
    lower = diff >= 0
    dmat = jnp.where(lower[None], jnp.exp(jnp.where(lower, diff, 0.0)[None] * log_g[:, None, None]), 0.0)
    xi = jnp.exp((i[None, :] + 1.0) * log_g[:, None])
    zeta = jnp.exp((C - 1.0 - i[None, :]) * log_g[:, None])
    g_chunk = jnp.exp(C * log_g)

    def step(R, inp):
        qb, kb, vb = inp
        inner = jnp.einsum('bhid,bhjd->bhij', qb, kb) * dmat[None]
        o = (jnp.einsum('bhij,bhje->bhie', inner, vb)
             + jnp.einsum('bhid,bhde->bhie', qb, R) * xi[None, :, :, None])
        R = R * g_chunk[None, :, None, None] + jnp.einsum('bhjd,bhje->bhde', kb * zeta[None, :, :, None], vb)
        return R, o

    R0 = jnp.zeros((B, RET_HEADS, RET_QK_DIM, RET_V_HEAD_DIM), jnp.float32)
    _, o = lax.scan(step, R0, (qc, kc, vc))
    o = o.transpose(1, 0, 3, 2, 4).reshape(B, S, RET_HEADS, RET_V_HEAD_DIM)
    o = rms_norm(o, head_norm_g).reshape(B, S, RET_V).astype(xn.dtype)
    return (jax.nn.silu(gate) * o) @ w_out


def shared_kv(h, kv_norm_g, w_kv, k_norm_g):
    B, S, _ = h.shape
    kv = rms_norm(h, kv_norm_g) @ w_kv
    k = kv[..., :D_MODEL].reshape(B, S, SB_HEADS, SB_HEAD_DIM)
    v = kv[..., D_MODEL:].reshape(B, S, SB_HEADS, SB_HEAD_DIM)
    k = rms_norm(k, k_norm_g).transpose(0, 2, 1, 3)
    v = v.transpose(0, 2, 1, 3)
    return k, v


def stick_breaking_attention(q, k, v):
    S = q.shape[2]
    scale = SB_HEAD_DIM ** -0.5
    qf, kf, vf = q.astype(jnp.float32), k.astype(jnp.float32), v.astype(jnp.float32)
    outs = []
    for blk in range(S // SB_BLOCK):
        start = blk * SB_BLOCK
        L = start + SB_BLOCK
        z = jnp.einsum('bhtd,bhsd->bhts', qf[:, :, start:L], kf[:, :, :L]) * scale
        t_idx = start + jnp.arange(SB_BLOCK)[:, None]
        s_idx = jnp.arange(L)[None, :]
        causal = s_idx < t_idx
        log_beta = jax.nn.log_sigmoid(z)
        log_1m = jnp.where(causal, jax.nn.log_sigmoid(-z), 0.0)
        after = lax.cumsum(log_1m, axis=3, reverse=True) - log_1m
        a = jnp.where(causal, jnp.exp(log_beta + after), 0.0)
        outs.append(jnp.einsum('bhts,bhsd->bhtd', a, vf[:, :, :L]))
    return jnp.concatenate(outs, axis=2)


def stick_breaking_layer(xn, w_q, q_norm_g, w_out, k, v):
    B, S, _ = xn.shape
    q = rms_norm((xn @ w_q).reshape(B, S, SB_HEADS, SB_HEAD_DIM), q_norm_g).transpose(0, 2, 1, 3)
    o = stick_breaking_attention(q, k, v)
    o = o.transpose(0, 2, 1, 3).reshape(B, S, D_MODEL).astype(xn.dtype)
    return o @ w_out


def peer_ffn(xn, w_q, subkeys, u, v):
    B, S, D = xn.shape
    T = B * S
    xt = xn.reshape(T, D)
    q = (xt @ w_q).reshape(T, PEER_HEADS, 2, PEER_HALF).astype(jnp.float32)
    sk = subkeys.astype(jnp.float32)
    s1 = jnp.einsum('thd,nd->thn', q[:, :, 0], sk[0])
    s2 = jnp.einsum('thd,nd->thn', q[:, :, 1], sk[1])
    v1, i1 = lax.top_k(s1, PEER_TOPK)
    v2, i2 = lax.top_k(s2, PEER_TOPK)
    cand = (v1[..., :, None] + v2[..., None, :]).reshape(T, PEER_HEADS, PEER_TOPK * PEER_TOPK)
    cidx = (i1[..., :, None] * PEER_N_KEYS + i2[..., None, :]).reshape(T, PEER_HEADS, PEER_TOPK * PEER_TOPK)
    top_s, pos = lax.top_k(cand, PEER_TOPK)
    eidx = jnp.take_along_axis(cidx, pos, axis=-1)
    gates = jax.nn.softmax(top_s, axis=-1)

    HK = PEER_HEADS * PEER_TOPK
    nb = T // PEER_TOKEN_BLOCK
    xb = xt.reshape(nb, PEER_TOKEN_BLOCK, D)
    eb = eidx.reshape(nb, PEER_TOKEN_BLOCK, HK)
    gb = gates.reshape(nb, PEER_TOKEN_BLOCK, HK)

    def block(args):
        xs, es, gs = args
        ue = jnp.take(u, es, axis=0)
        act = jnp.einsum('tkd,td->tk', ue, xs).astype(jnp.float32)
        w = (gs * jax.nn.gelu(act, approximate=False)).astype(xs.dtype)
        ve = jnp.take(v, es, axis=0)
        return jnp.einsum('tk,tkd->td', w, ve)

    out = lax.map(block, (xb, eb, gb))
    return out.reshape(B, S, D)


def setup_inputs(seed: int = 0) -> dict:
    key = jax.random.key(seed)
    ks = jax.random.split(key, 20)
    f32 = jnp.float32
    D = D_MODEL
    def nrm(k, shape, scale):
        return jax.random.normal(k, shape, f32) * scale
    def gain(k, shape):
        return 1.0 + 0.02 * jax.random.normal(k, shape, f32)
    return {
        "x": jax.random.normal(ks[0], (BATCH, SEQ, D), f32),
        "attn_norm_g": gain(ks[1], (DEPTH, D)),
        "ret_w_in": nrm(ks[2], (N_A_LAYERS, D, RET_IN), D ** -0.5),
        "ret_head_norm_g": gain(ks[3], (N_A_LAYERS, RET_HEADS, RET_V_HEAD_DIM)),
        "ret_w_out": nrm(ks[4], (N_A_LAYERS, RET_V, D), RET_V ** -0.5),
        "kv_norm_g": gain(ks[5], (D,)),
        "w_kv": nrm(ks[6], (D, 2 * D), D ** -0.5),
        "k_norm_g": gain(ks[7], (SB_HEAD_DIM,)),
        "sb_w_q": nrm(ks[8], (N_B_LAYERS, D, D), D ** -0.5),
        "q_norm_g": gain(ks[9], (N_B_LAYERS, SB_HEAD_DIM)),
        "sb_w_out": nrm(ks[10], (N_B_LAYERS, D, D), D ** -0.5),
        "ffn_norm_g": gain(ks[11], (DEPTH, D)),
        "peer_w_q": nrm(ks[12], (DEPTH, D, PEER_HEADS * PEER_QUERY_DIM), D ** -0.5),
        "peer_subkeys": nrm(ks[13], (DEPTH, 2, PEER_N_KEYS, PEER_HALF), PEER_HALF ** -0.5),
        "peer_u": nrm(ks[14], (DEPTH, PEER_EXPERTS, D), D ** -0.5),
        "peer_v": nrm(ks[15], (DEPTH, PEER_EXPERTS, D), 0.25),
    }


def reference(x, attn_norm_g, ret_w_in, ret_head_norm_g, ret_w_out, kv_norm_g, w_kv, k_norm_g,
              sb_w_q, q_norm_g, sb_w_out, ffn_norm_g, peer_w_q, peer_subkeys, peer_u, peer_v):
    h = x
    k_sh = None
    v_sh = None
    for layer in range(DEPTH):
        xn = rms_norm(h, attn_norm_g[layer])
        if layer < N_A_LAYERS:
            a = layer
            h = h + retention_layer(xn, ret_w_in[a], ret_head_norm_g[a], ret_w_out[a])
        else:
            if layer == N_A_LAYERS:
                k_sh, v_sh = shared_kv(h, kv_norm_g, w_kv, k_norm_g)
                xn = rms_norm(h, attn_norm_g[layer])
            b = layer - N_A_LAYERS
            h = h + stick_breaking_layer(xn, sb_w_q[b], q_norm_g[b], sb_w_out[b], k_sh, v_sh)
        h = h + peer_ffn(rms_norm(h, ffn_norm_g[layer]), peer_w_q[layer], peer_subkeys[layer],
                         peer_u[layer], peer_v[layer])
    return h
```

```python
import functools
import math

import jax
import jax.numpy as jnp
from jax import lax
from jax.experimental import pallas as pl
from jax.experimental.pallas import tpu as pltpu

F32 = jnp.float32
BF16 = jnp.bfloat16

NORM_EPS = 1e-6
ROPE_BASE = 10000.0
RET_CHUNK = 128
PEER_TOPK = 16
V7X_VMEM_BYTES = 64 * 1024 * 1024
VMEM_RESERVE_BYTES = 6 * 1024 * 1024
NEG_INF = float("-inf")
INV_SQRT2 = 1.0 / math.sqrt(2.0)


def _vmem_limit(estimate_bytes):
    return int(min(max(estimate_bytes, 16 * 1024 * 1024), V7X_VMEM_BYTES - VMEM_RESERVE_BYTES))


def _tile(n, pref):
    t = min(n, pref)
    assert n % t == 0, (n, pref)
    return t


def _norm_kernel(*refs, n_add, n_gain, emit_sum):
    adds = refs[:n_add]
    g_ref = refs[n_add] if n_gain else None
    outs = refs[n_add + (1 if n_gain else 0):]
    s = adds[0][...]
    for r in adds[1:]:
        s = s + r[...]
    oi = 0
    if emit_sum:
        outs[0][...] = s
        oi = 1
    if n_gain:
        y = s * lax.rsqrt(jnp.mean(s * s, axis=-1, keepdims=True) + NORM_EPS)
        for gi in range(n_gain):
            outs[oi + gi][...] = (y * g_ref[gi:gi + 1, :]).astype(BF16)


def add_norm(addends, gains, *, emit_sum, tm=128):
    T, D = addends[0].shape
    tm = _tile(T, tm)
    n_add = len(addends)
    n_gain = 0 if gains is None else gains.shape[0]
    row = pl.BlockSpec((tm, D), lambda i: (i, 0))
    in_specs = [row] * n_add
    args = list(addends)
    if n_gain:
        in_specs.append(pl.BlockSpec((n_gain, D), lambda i: (0, 0)))
        args.append(gains)
    out_shape, out_specs = [], []
    if emit_sum:
        out_shape.append(jax.ShapeDtypeStruct((T, D), F32))
        out_specs.append(row)
    for _ in range(n_gain):
        out_shape.append(jax.ShapeDtypeStruct((T, D), BF16))
        out_specs.append(row)
    est = 2 * tm * D * (4 * n_add + 4 * int(emit_sum) + 2 * n_gain) + 4 * tm * D * 4
    return pl.pallas_call(
        functools.partial(_norm_kernel, n_add=n_add, n_gain=n_gain, emit_sum=emit_sum),
        grid=(T // tm,),
        in_specs=in_specs,
        out_specs=out_specs,
        out_shape=out_shape,
        compiler_params=pltpu.CompilerParams(
            dimension_semantics=("parallel",), vmem_limit_bytes=_vmem_limit(est)),
        name="add_norm",
    )(*args)


def _mm_kernel(*refs, nk, n_res):
    a_ref, w_ref = refs[0], refs[1]
    res_refs = refs[2:2 + n_res]
    o_ref = refs[2 + n_res]
    if nk == 1:
        acc = jnp.dot(a_ref[...], w_ref[...], preferred_element_type=F32)
        for r in res_refs:
            acc = acc + r[...]
        o_ref[...] = acc.astype(o_ref.dtype)
        return
    acc_ref = refs[3 + n_res]
    k = pl.program_id(2)

    @pl.when(k == 0)
    def _():
        acc_ref[...] = jnp.zeros_like(acc_ref)

    acc_ref[...] += jnp.dot(a_ref[...], w_ref[...], preferred_element_type=F32)

    @pl.when(k == nk - 1)
    def _():
        acc = acc_ref[...]
        for r in res_refs:
            acc = acc + r[...]
        o_ref[...] = acc.astype(o_ref.dtype)


def matmul(a, w, *, residuals=(), out_dtype=BF16, tm=1024, tn=1024, tk=4096):
    M, K = a.shape
    N = w.shape[1]
    tm, tn, tk = _tile(M, tm), _tile(N, tn), _tile(K, tk)
    nk = K // tk
    n_res = len(residuals)
    in_specs = [pl.BlockSpec((tm, tk), lambda i, j, k: (i, k)),
                pl.BlockSpec((tk, tn), lambda i, j, k: (k, j))]
    in_specs += [pl.BlockSpec((tm, tn), lambda i, j, k: (i, j))] * n_res
    osz = jnp.dtype(out_dtype).itemsize
    est = 2 * (tm * tk * 2 + tk * tn * 2 + tm * tn * (osz + 4 * n_res)) + 2 * tm * tn * 4
    return pl.pallas_call(
        functools.partial(_mm_kernel, nk=nk, n_res=n_res),
        grid=(M // tm, N // tn, nk),
        in_specs=in_specs,
        out_specs=pl.BlockSpec((tm, tn), lambda i, j, k: (i, j)),
        out_shape=jax.ShapeDtypeStruct((M, N), out_dtype),
        scratch_shapes=[pltpu.VMEM((tm, tn), F32)] if nk > 1 else [],
        compiler_params=pltpu.CompilerParams(
            dimension_semantics=("parallel", "parallel", "arbitrary"),
            vmem_limit_bytes=_vmem_limit(est)),
        name="matmul",
    )(a, w, *residuals)


def _ret_kernel(q_ref, k_ref, v_ref, gate_ref, cos_ref, sin_ref, dmat_ref, xi_ref, zeta_ref,
                gch_ref, hg_ref, o_ref, r_ref, *, n_chunks, chunk, dk):
    @pl.when(pl.program_id(2) == 0)
    def _():
        r_ref[...] = jnp.zeros_like(r_ref)

    half = dk // 2
    kscale = dk ** -0.5

    def rot(t, cos, sin):
        t1, t2 = t[:, :half], t[:, half:]
        return jnp.concatenate([t1 * cos - t2 * sin, t1 * sin + t2 * cos], axis=-1)

    def body(c, carry):
        rows = pl.ds(pl.multiple_of(c * chunk, chunk), chunk)
        cos, sin = cos_ref[rows, :], sin_ref[rows, :]
        q = rot(q_ref[rows, :].astype(F32), cos, sin)
        k = rot(k_ref[rows, :].astype(F32), cos, sin) * kscale
        vb = v_ref[rows, :]
        inner = lax.dot_general(q.astype(BF16), k.astype(BF16), (((1,), (1,)), ((), ())),
                                preferred_element_type=F32) * dmat_ref[...]
        r_old = r_ref[...]
        o = (jnp.dot(inner.astype(BF16), vb, preferred_element_type=F32)
             + jnp.dot((q * xi_ref[...]).astype(BF16), r_old.astype(BF16),
                       preferred_element_type=F32))
        kz = (k * zeta_ref[...]).astype(BF16)
        r_ref[...] = r_old * gch_ref[...] + lax.dot_general(
            kz, vb, (((0,), (0,)), ((), ())), preferred_element_type=F32)
        on = o * lax.rsqrt(jnp.mean(o * o, axis=-1, keepdims=True) + NORM_EPS) * hg_ref[...]
        g = gate_ref[rows, :].astype(F32)
        silu = g / (1.0 + jnp.exp(-g))
        o_ref[rows, :] = (silu * on).astype(o_ref.dtype)
        return carry

    lax.fori_loop(0, n_chunks, body, 0)


def retention(proj, head_norm_g, *, batch, seq, rows_per_step=1024):
    T, _ = proj.shape
    H, dv = head_norm_g.shape
    dk = dv // 2
    C = RET_CHUNK
    sb = _tile(seq, rows_per_step)
    ns = seq // sb

    inv = 1.0 / (ROPE_BASE ** (jnp.arange(0, dk, 2, dtype=F32) / dk))
    ang = jnp.arange(seq, dtype=F32)[:, None] * inv[None, :]
    cos, sin = jnp.cos(ang), jnp.sin(ang)
    log_g = jnp.log1p(-jnp.exp2(-5.0 - jnp.arange(H, dtype=F32)))
    i = jnp.arange(C, dtype=F32)
    diff = i[:, None] - i[None, :]
    lower = diff >= 0
    dmat = jnp.where(lower[None], jnp.exp(jnp.where(lower, diff, 0.0)[None] * log_g[:, None, None]), 0.0)
    xi = jnp.exp((i[None, :] + 1.0) * log_g[:, None])[:, :, None]
    zeta = jnp.exp((C - 1.0 - i[None, :]) * log_g[:, None])[:, :, None]
    gch = jnp.broadcast_to(jnp.exp(C * log_g)[:, None, None], (H, 1, dv))
    hg = head_norm_g.reshape(H, 1, dv).astype(F32)

    nqk = H * dk // dk
    row_idx = lambda b, h, s: b * ns + s
    in_specs = [
        pl.BlockSpec((sb, dk), lambda b, h, s: (row_idx(b, h, s), h)),
        pl.BlockSpec((sb, dk), lambda b, h, s: (row_idx(b, h, s), nqk + h)),
        pl.BlockSpec((sb, dv), lambda b, h, s: (row_idx(b, h, s), (2 * H * dk) // dv + h)),
        pl.BlockSpec((sb, dv), lambda b, h, s: (row_idx(b, h, s), (2 * H * dk) // dv + H + h)),
        pl.BlockSpec((sb, dk // 2), lambda b, h, s: (s, 0)),
        pl.BlockSpec((sb, dk // 2), lambda b, h, s: (s, 0)),
        pl.BlockSpec((None, C, C), lambda b, h, s: (h, 0, 0)),
        pl.BlockSpec((None, C, 1), lambda b, h, s: (h, 0, 0)),
        pl.BlockSpec((None, C, 1), lambda b, h, s: (h, 0, 0)),
        pl.BlockSpec((None, 1, dv), lambda b, h, s: (h, 0, 0)),
        pl.BlockSpec((None, 1, dv), lambda b, h, s: (h, 0, 0)),
    ]
    est = 2 * sb * (2 * dk * 2 + 3 * dv * 2 + dk * 4) + 8 * dk * dv * 4 + (1 << 22)
    return pl.pallas_call(
        functools.partial(_ret_kernel, n_chunks=sb // C, chunk=C, dk=dk),
        grid=(batch, H, ns),
        in_specs=in_specs,
        out_specs=pl.BlockSpec((sb, dv), lambda b, h, s: (row_idx(b, h, s), h)),
        out_shape=jax.ShapeDtypeStruct((T, H * dv), BF16),
        scratch_shapes=[pltpu.VMEM((dk, dv), F32)],
        compiler_params=pltpu.CompilerParams(
            dimension_semantics=("parallel", "parallel", "arbitrary"),
            vmem_limit_bytes=_vmem_limit(est)),
        name="retention",
    )(proj, proj, proj, proj, cos, sin, dmat, xi, zeta, gch, hg)


def _peer_pairs(topk):
    return [(r1, r2) for r1 in range(topk) for r2 in range(topk) if (r1 + 1) * (r2 + 1) <= topk]


def _first_max(work, iota, n):
    m = jnp.max(work, axis=0, keepdims=True)
    idx = jnp.min(jnp.where(work == m, iota, n), axis=0, keepdims=True)
    return m, iota == idx


def _route_kernel(q_ref, sk_ref, a1_ref, a2_ref, rows_ref, vals_ref, cand_ref, *, n_heads, topk):
    n_keys, half = sk_ref.shape[1], sk_ref.shape[2]
    tm = q_ref.shape[0]
    pairs = _peer_pairs(topk)
    n_cand = cand_ref.shape[0]
    iota_k = lax.broadcasted_iota(jnp.int32, (n_keys, tm), 0)
    iota_c = lax.broadcasted_iota(jnp.int32, (n_cand, tm), 0)
    a_refs = (a1_ref, a2_ref)
    cand_ref[...] = jnp.full(cand_ref.shape, NEG_INF, F32)
    for h in range(n_heads):
        for side in range(2):
            col = (2 * h + side) * half
            s_t = lax.dot_general(sk_ref[side], q_ref[:, col:col + half], (((1,), (1,)), ((), ())),
                                  preferred_element_type=F32)
            work = s_t
            for r in range(topk):
                m, pick = _first_max(work, iota_k, n_keys)
                vals_ref[side, r:r + 1, :] = m
                work = jnp.where(pick, NEG_INF, work)
            a_refs[side][h] = jnp.where(work == NEG_INF, s_t, NEG_INF)
        for p, (r1, r2) in enumerate(pairs):
            cand_ref[p:p + 1, :] = vals_ref[0, r1:r1 + 1, :] + vals_ref[1, r2:r2 + 1, :]
        work = cand_ref[...]
        cmax = None
        z = jnp.zeros((1, tm), F32)
        for r in range(topk):
            m, pick = _first_max(work, iota_c, n_cand)
            if r == 0:
                cmax = m
            z = z + jnp.exp(m - cmax)
            work = jnp.where(pick, NEG_INF, work)
        rows_ref[h, 0:1, :] = m
        rows_ref[h, 1:2, :] = vals_ref[0, 0:1, :]
        rows_ref[h, 2:3, :] = vals_ref[1, 0:1, :]
        rows_ref[h, 3:4, :] = 1.0 / z
        rows_ref[h, 4:8, :] = jnp.zeros((4, tm), F32)


def peer_route(q, subkeys, *, tm=512):
    T = q.shape[0]
    _, n_keys, half = subkeys.shape
    n_heads = q.shape[1] // (2 * half)
    tm = _tile(T, tm)
    n_cand = -(-len(_peer_pairs(PEER_TOPK)) // 8) * 8
    tab = pl.BlockSpec((n_heads, n_keys, tm), lambda i: (0, 0, i))
    est = 2 * (tm * q.shape[1] * 2 + 2 * n_heads * n_keys * tm * 4) + (1 << 23)
    return pl.pallas_call(
        functools.partial(_route_kernel, n_heads=n_heads, topk=PEER_TOPK),
        grid=(T // tm,),
        in_specs=[pl.BlockSpec((tm, q.shape[1]), lambda i: (i, 0)),
                  pl.BlockSpec(subkeys.shape, lambda i: (0, 0, 0))],
        out_specs=[tab, tab, pl.BlockSpec((n_heads, 8, tm), lambda i: (0, 0, i))],
        out_shape=[jax.ShapeDtypeStruct((n_heads, n_keys, T), F32),
                   jax.ShapeDtypeStruct((n_heads, n_keys, T), F32),
                   jax.ShapeDtypeStruct((n_heads, 8, T), F32)],
        scratch_shapes=[pltpu.VMEM((2, PEER_TOPK, tm), F32), pltpu.VMEM((n_cand, tm), F32)],
        compiler_params=pltpu.CompilerParams(
            dimension_semantics=("parallel",), vmem_limit_bytes=_vmem_limit(est)),
        name="peer_route",
    )(q, subkeys)


def _expert_kernel(xn_ref, u_ref, v_ref, a1_ref, a2_ref, rows_ref, o_ref, act_ref, w_ref, p2_ref,
                   *, n_heads, n_keys, n_i, lane_chunk, d_chunk):
    j = pl.program_id(1)
    tm, D = xn_ref.shape

    @pl.when(j == 0)
    def _():
        o_ref[...] = jnp.zeros_like(o_ref)
        for h in range(n_heads):
            p2_ref[h] = jnp.exp(a2_ref[h] - rows_ref[h, 2:3, :])

    act_ref[...] = lax.dot_general(u_ref[...], xn_ref[...], (((1,), (1,)), ((), ())),
                                   preferred_element_type=F32)
    i0 = j * n_i
    for il in range(n_i):
        rows_e = slice(il * n_keys, (il + 1) * n_keys)
        for lc in range(tm // lane_chunk):
            ls = slice(lc * lane_chunk, (lc + 1) * lane_chunk)
            gate = jnp.zeros((n_keys, lane_chunk), F32)
            for h in range(n_heads):
                a1row = a1_ref[h, pl.ds(i0 + il, 1), ls]
                p1row = jnp.exp(a1row - rows_ref[h, 1:2, ls]) * rows_ref[h, 3:4, ls]
                cand = a2_ref[h, :, ls] + a1row
                gate = gate + jnp.where(cand >= rows_ref[h, 0:1, ls], p2_ref[h, :, ls] * p1row, 0.0)
            a = act_ref[rows_e, ls]
            gelu = 0.5 * a * (1.0 + lax.erf(a * INV_SQRT2))
            w_ref[rows_e, ls] = (gate * gelu).astype(BF16)
    w = w_ref[...]
    for dc in range(D // d_chunk):
        ds_ = slice(dc * d_chunk, (dc + 1) * d_chunk)
        o_ref[:, ds_] += lax.dot_general(w, v_ref[:, ds_], (((0,), (0,)), ((), ())),
                                         preferred_element_type=F32)


def peer_experts(xn, u, v, a1, a2, rows, *, tm=512, te=512, lane_chunk=256, d_chunk=1024):
    T, D = xn.shape
    E = u.shape[0]
    n_heads, n_keys, _ = a1.shape
    tm, te = _tile(T, tm), _tile(E, te)
    assert te % n_keys == 0
    lane_chunk, d_chunk = _tile(tm, lane_chunk), _tile(D, d_chunk)
    once = dict(pipeline_mode=pl.Buffered(1))
    tab = pl.BlockSpec((n_heads, n_keys, tm), lambda i, j: (0, 0, i), **once)
    est = (tm * D * 2 + 2 * 2 * te * D * 2 + 2 * n_heads * n_keys * tm * 4 + 2 * tm * D * 4
           + te * tm * 6 + n_heads * n_keys * tm * 4 + 2 * tm * d_chunk * 4 + (1 << 22))
    return pl.pallas_call(
        functools.partial(_expert_kernel, n_heads=n_heads, n_keys=n_keys, n_i=te // n_keys,
                          lane_chunk=lane_chunk, d_chunk=d_chunk),
        grid=(T // tm, E // te),
        in_specs=[pl.BlockSpec((tm, D), lambda i, j: (i, 0), **once),
                  pl.BlockSpec((te, D), lambda i, j: (j, 0)),
                  pl.BlockSpec((te, D), lambda i, j: (j, 0)),
                  tab, tab,
                  pl.BlockSpec((n_heads, 8, tm), lambda i, j: (0, 0, i))],
        out_specs=pl.BlockSpec((tm, D), lambda i, j: (i, 0)),
        out_shape=jax.ShapeDtypeStruct((T, D), F32),
        scratch_shapes=[pltpu.VMEM((te, tm), F32), pltpu.VMEM((te, tm), BF16),
                        pltpu.VMEM((n_heads, n_keys, tm), F32)],
        compiler_params=pltpu.CompilerParams(
            dimension_semantics=("parallel", "arbitrary"), vmem_limit_bytes=_vmem_limit(est)),
        name="peer_experts",
    )(xn, u, v, a1, a2, rows)


def peer_ffn(xn, w_q, subkeys, u, v):
    q = matmul(xn, w_q)
    a1, a2, rows = peer_route(q, subkeys)
    return peer_experts(xn, u, v, a1, a2, rows)


def _head_norm(t, g, scale):
    t = t.astype(F32)
    return t * lax.rsqrt(jnp.mean(t * t, axis=-1, keepdims=True) + NORM_EPS) * (g * scale)


def _sb_kernel(q_ref, k_ref, v_ref, qg_ref, kg_ref, o_ref, kn_ref, acc_ref, carry_ref, *, tq, tk):
    qi = pl.program_id(2)
    dh = q_ref.shape[1]
    seq = k_ref.shape[0]

    @pl.when(qi == 0)
    def _():
        def norm_rows(c, carry):
            rows = pl.ds(pl.multiple_of(c * tq, tq), tq)
            kn_ref[rows, :] = _head_norm(k_ref[rows, :], kg_ref[...], 1.0).astype(BF16)
            return carry
        lax.fori_loop(0, seq // tq, norm_rows, 0)

    qn = _head_norm(q_ref[...], qg_ref[...], dh ** -0.5).astype(BF16)
    acc_ref[...] = jnp.zeros_like(acc_ref)
    carry_ref[...] = jnp.zeros_like(carry_ref)
    jj = lax.broadcasted_iota(jnp.int32, (tk, tk), 0)
    ss = lax.broadcasted_iota(jnp.int32, (tk, tk), 1)
    upper = (jj > ss).astype(BF16)

    def block(kb, masked):
        rows = pl.ds(pl.multiple_of(kb * tk, tk), tk)
        z = lax.dot_general(qn, kn_ref[rows, :], (((1,), (1,)), ((), ())), preferred_element_type=F32)
        l1m = jnp.minimum(-z, 0.0) - jnp.log(1.0 + jnp.exp(-jnp.abs(z)))
        if masked:
            t_idx = qi * tq + lax.broadcasted_iota(jnp.int32, (tq, tk), 0)
            s_idx = kb * tk + lax.broadcasted_iota(jnp.int32, (tq, tk), 1)
            causal = s_idx < t_idx
            l1m = jnp.where(causal, l1m, 0.0)
        hi = l1m.astype(BF16)
        lo = (l1m - hi.astype(F32)).astype(BF16)
        after = (jnp.dot(hi, upper, preferred_element_type=F32)
                 + jnp.dot(lo, upper, preferred_element_type=F32))
        carry = carry_ref[...]
        a = jnp.exp(z + l1m + after + carry)
        if masked:
            a = jnp.where(causal, a, 0.0)
        acc_ref[...] += jnp.dot(a.astype(BF16), v_ref[rows, :], preferred_element_type=F32)
        carry_ref[...] = carry + (after[:, 0:1] + l1m[:, 0:1])

    n_diag = tq // tk
    for d in range(n_diag):
        block(qi * n_diag + (n_diag - 1 - d), True)

    def loop_body(it, c):
        block(qi * n_diag - 1 - it, False)
        return c

    lax.fori_loop(0, qi * n_diag, loop_body, 0)
    o_ref[...] = acc_ref[...].astype(o_ref.dtype)


def sb_attention(q, kv, q_norm_g, k_norm_g, *, batch, seq, tq=512, tk=256):
    T, HD = q.shape
    dh = q_norm_g.shape[-1]
    H = HD // dh
    tq = _tile(seq, tq)
    tk = _tile(tq, tk)
    nq = seq // tq
    qg = q_norm_g.reshape(1, dh).astype(F32)
    kg = k_norm_g.reshape(1, dh).astype(F32)
    est = 4 * seq * dh * 2 + seq * dh * 2 + 16 * tq * tk * 4 + (1 << 22)
    return pl.pallas_call(
        functools.partial(_sb_kernel, tq=tq, tk=tk),
        grid=(batch, H, nq),
        in_specs=[pl.BlockSpec((tq, dh), lambda b, h, i: (b * nq + i, h)),
                  pl.BlockSpec((seq, dh), lambda b, h, i: (b, h)),
                  pl.BlockSpec((seq, dh), lambda b, h, i: (b, H + h)),
                  pl.BlockSpec((1, dh), lambda b, h, i: (0, 0)),
                  pl.BlockSpec((1, dh), lambda b, h, i: (0, 0))],
        out_specs=pl.BlockSpec((tq, dh), lambda b, h, i: (b * nq + i, h)),
        out_shape=jax.ShapeDtypeStruct((T, HD), BF16),
        scratch_shapes=[pltpu.VMEM((seq, dh), BF16), pltpu.VMEM((tq, dh), F32),
                        pltpu.VMEM((tq, 1), F32)],
        compiler_params=pltpu.CompilerParams(
            dimension_semantics=("parallel", "parallel", "arbitrary"),
            vmem_limit_bytes=_vmem_limit(est)),
        name="sb_attention",
    )(q, kv, kv, qg, kg)


def kernel(x, attn_norm_g, ret_w_in, ret_head_norm_g, ret_w_out, kv_norm_g, w_kv, k_norm_g, sb_w_q, q_norm_g,
           sb_w_out, ffn_norm_g, peer_w_q, peer_subkeys, peer_u, peer_v):
    B, S, D = x.shape
    T = B * S
    depth = attn_norm_g.shape[0]
    n_a = ret_w_in.shape[0]
    bf = lambda t: t.astype(BF16)
    h = x.reshape(T, D)
    pending = None
    for layer in range(depth):
        addends = [h] if pending is None else [h, pending]
        if layer < n_a:
            outs = add_norm(addends, attn_norm_g[layer:layer + 1], emit_sum=pending is not None)
            if pending is not None:
                h = outs[0]
            xn = outs[-1]
            proj = matmul(xn, bf(ret_w_in[layer]))
            o = retention(proj, ret_head_norm_g[layer], batch=B, seq=S)
            h = matmul(o, bf(ret_w_out[layer]), residuals=(h,), out_dtype=F32, tk=2048)
        else:
            b = layer - n_a
            gains = jnp.stack([kv_norm_g, attn_norm_g[layer]]) if layer == n_a else attn_norm_g[layer:layer + 1]
            outs = add_norm(addends, gains, emit_sum=pending is not None)
            if pending is not None:
                h = outs[0]
            if layer == n_a:
                kv = matmul(outs[-2], bf(w_kv))
            xn = outs[-1]
            q = matmul(xn, bf(sb_w_q[b]))
            o = sb_attention(q, kv, q_norm_g[b], k_norm_g, batch=B, seq=S)
            h = matmul(o, bf(sb_w_out[b]), residuals=(h,), out_dtype=F32, tk=2048)
        (fn,) = add_norm([h], ffn_norm_g[layer:layer + 1], emit_sum=False)
        pending = peer_ffn(fn, bf(peer_w_q[layer]), bf(peer_subkeys[layer]), bf(peer_u[layer]), bf(peer_v[layer]))
    (out,) = add_norm([h, pending], None, emit_sum=True)
    return out.reshape(B, S, D)
```

```python
import functools
import math

import jax
import jax.numpy as jnp
from jax import lax
from jax.experimental import pallas as pl
from jax.experimental.pallas import tpu as pltpu

F32 = jnp.float32
BF16 = jnp.bfloat16

NORM_EPS = 1e-6
ROPE_BASE = 10000.0
RET_CHUNK = 128
PEER_TOPK = 16
V7X_VMEM_BYTES = 64 * 1024 * 1024
VMEM_RESERVE_BYTES = 6 * 1024 * 1024
NEG_INF = float("-inf")
INV_SQRT2 = 1.0 / math.sqrt(2.0)
LOG2E = 1.0 / math.log(2.0)


def _vmem_limit(estimate_bytes):
    return int(min(max(estimate_bytes, 16 * 1024 * 1024), V7X_VMEM_BYTES - VMEM_RESERVE_BYTES))


def _tile(n, pref):
    t = min(n, pref)
    assert n % t == 0, (n, pref)
    return t


def _norm_kernel(*refs, n_add, n_gain, emit_sum):
    adds = refs[:n_add]
    g_ref = refs[n_add] if n_gain else None
    outs = refs[n_add + (1 if n_gain else 0):]
    s = adds[0][...]
    for r in adds[1:]:
        s = s + r[...]
    oi = 0
    if emit_sum:
        outs[0][...] = s
        oi = 1
    if n_gain:
        y = s * lax.rsqrt(jnp.mean(s * s, axis=-1, keepdims=True) + NORM_EPS)
        for gi in range(n_gain):
            outs[oi + gi][...] = (y * g_ref[gi:gi + 1, :]).astype(BF16)


def add_norm(addends, gains, *, emit_sum, tm=128):
    T, D = addends[0].shape
    tm = _tile(T, tm)
    n_add = len(addends)
    n_gain = 0 if gains is None else gains.shape[0]
    row = pl.BlockSpec((tm, D), lambda i: (i, 0))
    in_specs = [row] * n_add
    args = list(addends)
    if n_gain:
        in_specs.append(pl.BlockSpec((n_gain, D), lambda i: (0, 0)))
        args.append(gains)
    out_shape, out_specs = [], []
    if emit_sum:
        out_shape.append(jax.ShapeDtypeStruct((T, D), F32))
        out_specs.append(row)
    for _ in range(n_gain):
        out_shape.append(jax.ShapeDtypeStruct((T, D), BF16))
        out_specs.append(row)
    est = 2 * tm * D * (4 * n_add + 4 * int(emit_sum) + 2 * n_gain) + 4 * tm * D * 4
    return pl.pallas_call(
        functools.partial(_norm_kernel, n_add=n_add, n_gain=n_gain, emit_sum=emit_sum),
        grid=(T // tm,),
        in_specs=in_specs,
        out_specs=out_specs,
        out_shape=out_shape,
        compiler_params=pltpu.CompilerParams(
            dimension_semantics=("parallel",), vmem_limit_bytes=_vmem_limit(est)),
        name="add_norm",
    )(*args)


def _mm_kernel(*refs, nk, n_res):
    a_ref, w_ref = refs[0], refs[1]
    res_refs = refs[2:2 + n_res]
    o_ref = refs[2 + n_res]
    if nk == 1:
        acc = jnp.dot(a_ref[...], w_ref[...], preferred_element_type=F32)
        for r in res_refs:
            acc = acc + r[...]
        o_ref[...] = acc.astype(o_ref.dtype)
        return
    acc_ref = refs[3 + n_res]
    k = pl.program_id(2)

    @pl.when(k == 0)
    def _():
        acc_ref[...] = jnp.zeros_like(acc_ref)

    acc_ref[...] += jnp.dot(a_ref[...], w_ref[...], preferred_element_type=F32)

    @pl.when(k == nk - 1)
    def _():
        acc = acc_ref[...]
        for r in res_refs:
            acc = acc + r[...]
        o_ref[...] = acc.astype(o_ref.dtype)


def matmul(a, w, *, residuals=(), out_dtype=BF16, tm=1024, tn=1024, tk=4096):
    M, K = a.shape
    N = w.shape[1]
    tm, tn, tk = _tile(M, tm), _tile(N, tn), _tile(K, tk)
    nk = K // tk
    n_res = len(residuals)
    in_specs = [pl.BlockSpec((tm, tk), lambda i, j, k: (i, k)),
                pl.BlockSpec((tk, tn), lambda i, j, k: (k, j))]
    in_specs += [pl.BlockSpec((tm, tn), lambda i, j, k: (i, j))] * n_res
    osz = jnp.dtype(out_dtype).itemsize
    est = 2 * (tm * tk * 2 + tk * tn * 2 + tm * tn * (osz + 4 * n_res)) + 2 * tm * tn * 4
    return pl.pallas_call(
        functools.partial(_mm_kernel, nk=nk, n_res=n_res),
        grid=(M // tm, N // tn, nk),
        in_specs=in_specs,
        out_specs=pl.BlockSpec((tm, tn), lambda i, j, k: (i, j)),
        out_shape=jax.ShapeDtypeStruct((M, N), out_dtype),
        scratch_shapes=[pltpu.VMEM((tm, tn), F32)] if nk > 1 else [],
        compiler_params=pltpu.CompilerParams(
            dimension_semantics=("parallel", "parallel", "arbitrary"),
            vmem_limit_bytes=_vmem_limit(est)),
        name="matmul",
    )(a, w, *residuals)


def _ret_kernel(q_ref, k_ref, v_ref, gate_ref, cos_ref, sin_ref, dmat_ref, xi_ref, zeta_ref,
                gch_ref, hg_ref, o_ref, r_ref, *, n_chunks, chunk, dk):
    @pl.when(pl.program_id(2) == 0)
    def _():
        r_ref[...] = jnp.zeros_like(r_ref)

    half = dk // 2
    kscale = dk ** -0.5

    def rot(t, cos, sin):
        t1, t2 = t[:, :half], t[:, half:]
        return jnp.concatenate([t1 * cos - t2 * sin, t1 * sin + t2 * cos], axis=-1)

    def body(c, carry):
        rows = pl.ds(pl.multiple_of(c * chunk, chunk), chunk)
        cos, sin = cos_ref[rows, :], sin_ref[rows, :]
        q = rot(q_ref[rows, :].astype(F32), cos, sin)
        k = rot(k_ref[rows, :].astype(F32), cos, sin) * kscale
        vb = v_ref[rows, :]
        inner = lax.dot_general(q.astype(BF16), k.astype(BF16), (((1,), (1,)), ((), ())),
                                preferred_element_type=F32) * dmat_ref[...]
        r_old = r_ref[...]
        o = (jnp.dot(inner.astype(BF16), vb, preferred_element_type=F32)
             + jnp.dot((q * xi_ref[...]).astype(BF16), r_old.astype(BF16),
                       preferred_element_type=F32))
        kz = (k * zeta_ref[...]).astype(BF16)
        r_ref[...] = r_old * gch_ref[...] + lax.dot_general(
            kz, vb, (((0,), (0,)), ((), ())), preferred_element_type=F32)
        on = o * lax.rsqrt(jnp.mean(o * o, axis=-1, keepdims=True) + NORM_EPS) * hg_ref[...]
        g = gate_ref[rows, :].astype(F32)
        silu = g / (1.0 + jnp.exp(-g))
        o_ref[rows, :] = (silu * on).astype(o_ref.dtype)
        return carry

    lax.fori_loop(0, n_chunks, body, 0)


def retention(proj, head_norm_g, *, batch, seq, rows_per_step=1024):
    T, _ = proj.shape
    H, dv = head_norm_g.shape
    dk = dv // 2
    C = RET_CHUNK
    sb = _tile(seq, rows_per_step)
    ns = seq // sb

    inv = 1.0 / (ROPE_BASE ** (jnp.arange(0, dk, 2, dtype=F32) / dk))
    ang = jnp.arange(seq, dtype=F32)[:, None] * inv[None, :]
    cos, sin = jnp.cos(ang), jnp.sin(ang)
    log_g = jnp.log1p(-jnp.exp2(-5.0 - jnp.arange(H, dtype=F32)))
    i = jnp.arange(C, dtype=F32)
    diff = i[:, None] - i[None, :]
    lower = diff >= 0
    dmat = jnp.where(lower[None], jnp.exp(jnp.where(lower, diff, 0.0)[None] * log_g[:, None, None]), 0.0)
    xi = jnp.exp((i[None, :] + 1.0) * log_g[:, None])[:, :, None]
    zeta = jnp.exp((C - 1.0 - i[None, :]) * log_g[:, None])[:, :, None]
    gch = jnp.broadcast_to(jnp.exp(C * log_g)[:, None, None], (H, 1, dv))
    hg = head_norm_g.reshape(H, 1, dv).astype(F32)

    nqk = H * dk // dk
    row_idx = lambda b, h, s: b * ns + s
    in_specs = [
        pl.BlockSpec((sb, dk), lambda b, h, s: (row_idx(b, h, s), h)),
        pl.BlockSpec((sb, dk), lambda b, h, s: (row_idx(b, h, s), nqk + h)),
        pl.BlockSpec((sb, dv), lambda b, h, s: (row_idx(b, h, s), (2 * H * dk) // dv + h)),
        pl.BlockSpec((sb, dv), lambda b, h, s: (row_idx(b, h, s), (2 * H * dk) // dv + H + h)),
        pl.BlockSpec((sb, dk // 2), lambda b, h, s: (s, 0)),
        pl.BlockSpec((sb, dk // 2), lambda b, h, s: (s, 0)),
        pl.BlockSpec((None, C, C), lambda b, h, s: (h, 0, 0)),
        pl.BlockSpec((None, C, 1), lambda b, h, s: (h, 0, 0)),
        pl.BlockSpec((None, C, 1), lambda b, h, s: (h, 0, 0)),
        pl.BlockSpec((None, 1, dv), lambda b, h, s: (h, 0, 0)),
        pl.BlockSpec((None, 1, dv), lambda b, h, s: (h, 0, 0)),
    ]
    est = 2 * sb * (2 * dk * 2 + 3 * dv * 2 + dk * 4) + 8 * dk * dv * 4 + (1 << 22)
    return pl.pallas_call(
        functools.partial(_ret_kernel, n_chunks=sb // C, chunk=C, dk=dk),
        grid=(batch, H, ns),
        in_specs=in_specs,
        out_specs=pl.BlockSpec((sb, dv), lambda b, h, s: (row_idx(b, h, s), h)),
        out_shape=jax.ShapeDtypeStruct((T, H * dv), BF16),
        scratch_shapes=[pltpu.VMEM((dk, dv), F32)],
        compiler_params=pltpu.CompilerParams(
            dimension_semantics=("parallel", "parallel", "arbitrary"),
            vmem_limit_bytes=_vmem_limit(est)),
        name="retention",
    )(proj, proj, proj, proj, cos, sin, dmat, xi, zeta, gch, hg)


def _peer_pairs(topk):
    return [(r1, r2) for r1 in range(topk) for r2 in range(topk) if (r1 + 1) * (r2 + 1) <= topk]


def _first_max(work, iota, n):
    m = jnp.max(work, axis=0, keepdims=True)
    idx = jnp.min(jnp.where(work == m, iota, n), axis=0, keepdims=True)
    return m, iota == idx


def _route_kernel(q_ref, sk_ref, n1_ref, p1_ref, r2_ref, p2_ref, s_ref, vals_ref, cand_ref, *, topk, lane_chunk):
    n_keys, half = sk_ref.shape[1], sk_ref.shape[2]
    tm = q_ref.shape[0]
    pairs = _peer_pairs(topk)
    n_cand = cand_ref.shape[0]
    iota_k = lax.broadcasted_iota(jnp.int32, (n_keys, lane_chunk), 0)
    iota_c = lax.broadcasted_iota(jnp.int32, (n_cand, lane_chunk), 0)
    for side in range(2):
        s_ref[side] = lax.dot_general(sk_ref[side], q_ref[:, side * half:(side + 1) * half],
                                      (((1,), (1,)), ((), ())), preferred_element_type=F32)
    cand_ref[...] = jnp.full(cand_ref.shape, NEG_INF, F32)
    for lc in range(tm // lane_chunk):
        ls = slice(lc * lane_chunk, (lc + 1) * lane_chunk)
        ranks = []
        for side in range(2):
            work = s_ref[side, :, ls]
            rank = jnp.full((n_keys, lane_chunk), float(topk), F32)
            for r in range(topk):
                m, pick = _first_max(work, iota_k, n_keys)
                vals_ref[side, r:r + 1, ls] = m
                work = jnp.where(pick, NEG_INF, work)
                rank = jnp.where(pick, float(r), rank)
            ranks.append(rank)
        for p, (r1, r2) in enumerate(pairs):
            cand_ref[p:p + 1, ls] = vals_ref[0, r1:r1 + 1, ls] + vals_ref[1, r2:r2 + 1, ls]
        work = cand_ref[:, ls]
        cmax = None
        z = jnp.zeros((1, lane_chunk), F32)
        for r in range(topk):
            m, pick = _first_max(work, iota_c, n_cand)
            if r == 0:
                cmax = m
            z = z + jnp.exp(m - cmax)
            work = jnp.where(pick, NEG_INF, work)
        taken = jnp.where((work == NEG_INF) & (iota_c < len(pairs)), 1.0, 0.0)
        n1 = jnp.zeros((n_keys, lane_chunk), F32)
        start = 0
        for r1 in range(topk):
            cnt = topk // (r1 + 1)
            n_r1 = jnp.sum(taken[start:start + cnt, :], axis=0, keepdims=True)
            n1 = jnp.where(ranks[0] == float(r1), n_r1, n1)
            start += cnt
        inv_z = 1.0 / z
        n1_ref[:, ls] = n1
        p1_ref[:, ls] = jnp.where(ranks[0] < float(topk),
                                  jnp.exp(s_ref[0, :, ls] - vals_ref[0, 0:1, ls]) * inv_z, 0.0)
        r2_ref[:, ls] = ranks[1].astype(BF16)
        p2_ref[:, ls] = jnp.where(ranks[1] < float(topk),
                                  jnp.exp(s_ref[1, :, ls] - vals_ref[1, 0:1, ls]), 0.0).astype(BF16)


def peer_route(q, subkeys, *, tm=512, lane_chunk=128):
    T = q.shape[0]
    _, n_keys, half = subkeys.shape
    n_heads = q.shape[1] // (2 * half)
    tm = _tile(T, tm)
    lane_chunk = _tile(tm, lane_chunk)
    n_cand = -(-len(_peer_pairs(PEER_TOPK)) // 8) * 8
    tab = pl.BlockSpec((None, n_keys, tm), lambda i, h: (h, 0, i))
    tab_shape = jax.ShapeDtypeStruct((n_heads, n_keys, T), F32)
    tab_bf = jax.ShapeDtypeStruct((n_heads, n_keys, T), BF16)
    est = 2 * (tm * 2 * half * 2 + 4 * n_keys * tm * 4) + (2 * n_keys + 2 * PEER_TOPK + n_cand) * tm * 4 + (1 << 23)
    return pl.pallas_call(
        functools.partial(_route_kernel, topk=PEER_TOPK, lane_chunk=lane_chunk),
        grid=(T // tm, n_heads),
        in_specs=[pl.BlockSpec((tm, 2 * half), lambda i, h: (i, h)),
                  pl.BlockSpec(subkeys.shape, lambda i, h: (0, 0, 0))],
        out_specs=[tab] * 4,
        out_shape=[tab_shape, tab_shape, tab_bf, tab_bf],
        scratch_shapes=[pltpu.VMEM((2, n_keys, tm), F32), pltpu.VMEM((2, PEER_TOPK, tm), F32),
                        pltpu.VMEM((n_cand, tm), F32)],
        compiler_params=pltpu.CompilerParams(
            dimension_semantics=("parallel", "parallel"), vmem_limit_bytes=_vmem_limit(est)),
        name="peer_route",
    )(q, subkeys)


def _expert_kernel(xn_ref, u_ref, v_ref, n1_ref, p1_ref, r2_ref, p2_ref, o_ref, act_ref, g_ref, w0_ref, w1_ref,
                   *, n_e, n_keys, n_i, lane_chunk, k_chunk, d_chunk):
    j = pl.program_id(1)
    tm, D = xn_ref.shape
    n_heads = n1_ref.shape[0]

    @pl.when(j == 0)
    def _():
        o_ref[...] = jnp.zeros_like(o_ref)
        w1_ref[...] = jnp.zeros_like(w1_ref)

    def gate_rows(il):
        i_row = jnp.minimum(j, n_e - 1) * n_i + il
        sub = 16
        for lc in range(tm // lane_chunk):
            ls = slice(lc * lane_chunk, (lc + 1) * lane_chunk)
            gate = jnp.zeros((n_keys // sub, sub, lane_chunk), BF16)
            for h in range(n_heads):
                n1b = jnp.broadcast_to(n1_ref[h, pl.ds(i_row, 1), :][:, ls], (sub, lane_chunk)).astype(BF16)
                p1b = jnp.broadcast_to(p1_ref[h, pl.ds(i_row, 1), :][:, ls], (sub, lane_chunk)).astype(BF16)
                r2 = r2_ref[h, :, ls].reshape(n_keys // sub, sub, lane_chunk)
                p2 = p2_ref[h, :, ls].reshape(n_keys // sub, sub, lane_chunk)
                gate = gate + jnp.where(r2 < n1b[None], p2 * p1b[None], jnp.zeros_like(p2))
            g_ref[il * n_keys:(il + 1) * n_keys, ls] = gate.reshape(n_keys, lane_chunk)

    @pl.when(j < n_e)
    def _():
        n_k = D // k_chunk
        acc = None
        for step in range(max(n_k, n_i)):
            if step < n_k:
                ks = slice(step * k_chunk, (step + 1) * k_chunk)
                part = lax.dot_general(u_ref[:, ks], xn_ref[:, ks], (((1,), (1,)), ((), ())),
                                       preferred_element_type=F32)
                acc = part if acc is None else acc + part
            if step < n_i:
                gate_rows(step)
        act_ref[...] = acc

    def mix(w_prev_ref, w_cur_ref):
        w_prev = w_prev_ref[...]
        n_d = D // d_chunk
        for step in range(max(n_d, n_i)):
            if step < n_d:
                ds_ = slice(step * d_chunk, (step + 1) * d_chunk)
                o_ref[:, ds_] += lax.dot_general(w_prev, v_ref[:, ds_], (((0,), (0,)), ((), ())),
                                                 preferred_element_type=F32)
            if step < n_i:
                rows_e = slice(step * n_keys, (step + 1) * n_keys)
                a = act_ref[rows_e, :]
                gelu = 0.5 * a * (1.0 + lax.erf(a * INV_SQRT2))
                w_cur_ref[rows_e, :] = g_ref[rows_e, :] * gelu.astype(BF16)

    parity = lax.rem(j, 2)
    pl.when(parity == 0)(lambda: mix(w1_ref, w0_ref))
    pl.when(parity == 1)(lambda: mix(w0_ref, w1_ref))


def peer_experts(xn, u, v, n1, p1, r2, p2, *, tm=512, te=512, lane_chunk=256, k_chunk=1024, d_chunk=1024):
    T, D = xn.shape
    E = u.shape[0]
    n_heads, n_keys, _ = n1.shape
    tm, te = _tile(T, tm), _tile(E, te)
    assert te % n_keys == 0
    n_e = E // te
    lane_chunk, k_chunk, d_chunk = _tile(tm, lane_chunk), _tile(D, k_chunk), _tile(D, d_chunk)
    once = dict(pipeline_mode=pl.Buffered(1))
    tab = pl.BlockSpec((n_heads, n_keys, tm), lambda i, j: (0, 0, i), **once)
    est = (tm * D * 2 + 2 * 2 * te * D * 2 + 4 * n_heads * n_keys * tm * 4 + 2 * tm * D * 4
           + te * tm * 10 + 2 * tm * d_chunk * 4 + (1 << 22))
    return pl.pallas_call(
        functools.partial(_expert_kernel, n_e=n_e, n_keys=n_keys, n_i=te // n_keys,
                          lane_chunk=lane_chunk, k_chunk=k_chunk, d_chunk=d_chunk),
        grid=(T // tm, n_e + 1),
        in_specs=[pl.BlockSpec((tm, D), lambda i, j: (i, 0), **once),
                  pl.BlockSpec((te, D), lambda i, j: (jnp.minimum(j, n_e - 1), 0)),
                  pl.BlockSpec((te, D), lambda i, j: (jnp.maximum(j - 1, 0), 0)),
                  tab, tab, tab, tab],
        out_specs=pl.BlockSpec((tm, D), lambda i, j: (i, 0)),
        out_shape=jax.ShapeDtypeStruct((T, D), F32),
        scratch_shapes=[pltpu.VMEM((te, tm), F32), pltpu.VMEM((te, tm), BF16),
                        pltpu.VMEM((te, tm), BF16), pltpu.VMEM((te, tm), BF16)],
        compiler_params=pltpu.CompilerParams(
            dimension_semantics=("parallel", "arbitrary"), vmem_limit_bytes=_vmem_limit(est)),
        name="peer_experts",
    )(xn, u, v, n1, p1, r2, p2)


def peer_ffn(xn, w_q, subkeys, u, v):
    q = matmul(xn, w_q)
    return peer_experts(xn, u, v, *peer_route(q, subkeys))


def _head_norm(t, g, scale):
    t = t.astype(F32)
    return t * lax.rsqrt(jnp.mean(t * t, axis=-1, keepdims=True) + NORM_EPS) * (g * scale)


def _sb_kernel(q_ref, k_ref, v_ref, qg_ref, kg_ref, o_ref, kn_ref, acc_ref, carry_ref, *, tq, tk, unroll):
    qi = pl.program_id(2)
    dh = q_ref.shape[1]
    seq = k_ref.shape[0]

    @pl.when(qi == 0)
    def _():
        def norm_rows(c, carry):
            rows = pl.ds(pl.multiple_of(c * tq, tq), tq)
            kn_ref[rows, :] = _head_norm(k_ref[rows, :], kg_ref[...], 1.0).astype(BF16)
            return carry
        lax.fori_loop(0, seq // tq, norm_rows, 0)

    qn = _head_norm(q_ref[...], qg_ref[...], dh ** -0.5 * LOG2E).astype(BF16)
    acc_ref[...] = jnp.zeros_like(acc_ref)
    carry_ref[...] = jnp.zeros_like(carry_ref)
    jj = lax.broadcasted_iota(jnp.int32, (tk, tk), 0)
    ss = lax.broadcasted_iota(jnp.int32, (tk, tk), 1)
    upper = (jj > ss).astype(BF16)

    def scores(kb, masked):
        rows = pl.ds(pl.multiple_of(kb * tk, tk), tk)
        z = lax.dot_general(qn, kn_ref[rows, :], (((1,), (1,)), ((), ())), preferred_element_type=F32)
        nz = -z
        l1m = jnp.minimum(nz, 0.0) - jnp.log2(1.0 + jnp.exp2(jnp.minimum(z, nz)))
        causal = None
        if masked:
            t_idx = qi * tq + lax.broadcasted_iota(jnp.int32, (tq, tk), 0)
            s_idx = kb * tk + lax.broadcasted_iota(jnp.int32, (tq, tk), 1)
            causal = s_idx < t_idx
            l1m = jnp.where(causal, l1m, 0.0)
        after = jnp.dot(l1m.astype(BF16), upper, preferred_element_type=F32)
        return rows, z + l1m + after, after[:, 0:1] + l1m[:, 0:1], causal

    def accumulate(rows, logit, total, causal):
        carry = carry_ref[...]
        a = jnp.exp2(logit + carry)
        if causal is not None:
            a = jnp.where(causal, a, 0.0)
        acc_ref[...] += jnp.dot(a.astype(BF16), v_ref[rows, :], preferred_element_type=F32)
        carry_ref[...] = carry + total

    n_diag = tq // tk
    for t in [scores(qi * n_diag + (n_diag - 1 - d), True) for d in range(n_diag)]:
        accumulate(*t)

    def loop_body(it, c):
        first = qi * n_diag - 1 - it * unroll
        terms = [scores(first - u, False) for u in range(unroll)]
        for t in terms:
            accumulate(*t)
        return c

    lax.fori_loop(0, (qi * n_diag) // unroll, loop_body, 0)
    o_ref[...] = acc_ref[...].astype(o_ref.dtype)


def sb_attention(q, kv, q_norm_g, k_norm_g, *, batch, seq, tq=512, tk=256, unroll=2):
    T, HD = q.shape
    dh = q_norm_g.shape[-1]
    H = HD // dh
    tq = _tile(seq, tq)
    tk = _tile(tq, tk)
    nq = seq // tq
    unroll = math.gcd(unroll, tq // tk)
    qg = q_norm_g.reshape(1, dh).astype(F32)
    kg = k_norm_g.reshape(1, dh).astype(F32)
    est = 4 * seq * dh * 2 + seq * dh * 2 + 16 * tq * tk * 4 + (1 << 22)
    return pl.pallas_call(
        functools.partial(_sb_kernel, tq=tq, tk=tk, unroll=unroll),
        grid=(batch, H, nq),
        in_specs=[pl.BlockSpec((tq, dh), lambda b, h, i: (b * nq + i, h)),
                  pl.BlockSpec((seq, dh), lambda b, h, i: (b, h)),
                  pl.BlockSpec((seq, dh), lambda b, h, i: (b, H + h)),
                  pl.BlockSpec((1, dh), lambda b, h, i: (0, 0)),
                  pl.BlockSpec((1, dh), lambda b, h, i: (0, 0))],
        out_specs=pl.BlockSpec((tq, dh), lambda b, h, i: (b * nq + i, h)),
        out_shape=jax.ShapeDtypeStruct((T, HD), BF16),
        scratch_shapes=[pltpu.VMEM((seq, dh), BF16), pltpu.VMEM((tq, dh), F32),
                        pltpu.VMEM((tq, 1), F32)],
        compiler_params=pltpu.CompilerParams(
            dimension_semantics=("parallel", "parallel", "arbitrary"),
            vmem_limit_bytes=_vmem_limit(est)),
        name="sb_attention",
    )(q, kv, kv, qg, kg)


def kernel(x, attn_norm_g, ret_w_in, ret_head_norm_g, ret_w_out, kv_norm_g, w_kv, k_norm_g, sb_w_q, q_norm_g,
           sb_w_out, ffn_norm_g, peer_w_q, peer_subkeys, peer_u, peer_v):
    B, S, D = x.shape
    T = B * S
    depth = attn_norm_g.shape[0]
    n_a = ret_w_in.shape[0]
    bf = lambda t: t.astype(BF16)
    h = x.reshape(T, D)
    pending = None
    for layer in range(depth):
        addends = [h] if pending is None else [h, pending]
        if layer < n_a:
            outs = add_norm(addends, attn_norm_g[layer:layer + 1], emit_sum=pending is not None)
            if pending is not None:
                h = outs[0]
            xn = outs[-1]
            proj = matmul(xn, bf(ret_w_in[layer]))
            o = retention(proj, ret_head_norm_g[layer], batch=B, seq=S)
            h = matmul(o, bf(ret_w_out[layer]), residuals=(h,), out_dtype=F32, tk=2048)
        else:
            b = layer - n_a
            gains = jnp.stack([kv_norm_g, attn_norm_g[layer]]) if layer == n_a else attn_norm_g[layer:layer + 1]
            outs = add_norm(addends, gains, emit_sum=pending is not None)
            if pending is not None:
                h = outs[0]
            if layer == n_a:
                kv = matmul(outs[-2], bf(w_kv))
            xn = outs[-1]
            q = matmul(xn, bf(sb_w_q[b]))
            o = sb_attention(q, kv, q_norm_g[b], k_norm_g, batch=B, seq=S)
            h = matmul(o, bf(sb_w_out[b]), residuals=(h,), out_dtype=F32, tk=2048)
        (fn,) = add_norm([h], ffn_norm_g[layer:layer + 1], emit_sum=False)
        pending = peer_ffn(fn, bf(peer_w_q[layer]), bf(peer_subkeys[layer]), bf(peer_u[layer]), bf(peer_v[layer]))
    (out,) = add_norm([h, pending], None, emit_sum=True)
    return out.reshape(B, S, D)
```

```python
import functools
import math

import jax
import jax.numpy as jnp
from jax import lax
from jax.experimental import pallas as pl
from jax.experimental.pallas import tpu as pltpu

F32 = jnp.float32
BF16 = jnp.bfloat16

NORM_EPS = 1e-6
ROPE_BASE = 10000.0
RET_CHUNK = 256
PEER_TOPK = 16
V7X_VMEM_BYTES = 64 * 1024 * 1024
VMEM_RESERVE_BYTES = 6 * 1024 * 1024
NEG_INF = float("-inf")
INV_SQRT2 = 1.0 / math.sqrt(2.0)
LOG2E = 1.0 / math.log(2.0)


def _vmem_limit(estimate_bytes):
    return int(min(max(estimate_bytes, 16 * 1024 * 1024), V7X_VMEM_BYTES - VMEM_RESERVE_BYTES))


def _tile(n, pref):
    t = min(n, pref)
    assert n % t == 0, (n, pref)
    return t


def _norm_kernel(x_ref, g_ref, *out_refs):
    x = x_ref[...]
    y = x * lax.rsqrt(jnp.mean(x * x, axis=-1, keepdims=True) + NORM_EPS)
    for gi, o_ref in enumerate(out_refs):
        o_ref[...] = (y * g_ref[gi:gi + 1, :]).astype(o_ref.dtype)


def rms_norms(x, gains, *, tm=256):
    T, D = x.shape
    n = gains.shape[0]
    tm = _tile(T, tm)
    row = pl.BlockSpec((tm, D), lambda i: (i, 0))
    est = 2 * tm * D * (4 + 2 * n) + 4 * tm * D * 4
    return pl.pallas_call(
        _norm_kernel,
        grid=(T // tm,),
        in_specs=[row, pl.BlockSpec((n, D), lambda i: (0, 0))],
        out_specs=[row] * n,
        out_shape=[jax.ShapeDtypeStruct((T, D), BF16)] * n,
        compiler_params=pltpu.CompilerParams(
            dimension_semantics=("parallel",), vmem_limit_bytes=_vmem_limit(est)),
        name="rms_norms",
    )(x, gains)


def _cast_kernel(x_ref, o_ref):
    o_ref[...] = x_ref[...].astype(o_ref.dtype)


def to_bf16(w, index=None, *, tr=512, tc=4096):
    R, C = w.shape[-2:]
    tr, tc = _tile(R, tr), _tile(C, tc)
    if index is None:
        in_spec = pl.BlockSpec((tr, tc), lambda i, j: (i, j))
    else:
        in_spec = pl.BlockSpec((None, tr, tc), lambda i, j: (index, i, j))
    return pl.pallas_call(
        _cast_kernel,
        grid=(R // tr, C // tc),
        in_specs=[in_spec],
        out_specs=pl.BlockSpec((tr, tc), lambda i, j: (i, j)),
        out_shape=jax.ShapeDtypeStruct((R, C), BF16),
        compiler_params=pltpu.CompilerParams(
            dimension_semantics=("parallel", "parallel"), vmem_limit_bytes=_vmem_limit(2 * tr * tc * 6 + (1 << 22))),
        name="to_bf16",
    )(w)


def _mm_kernel(*refs, nk, n_res):
    a_ref, w_ref = refs[0], refs[1]
    res_refs = refs[2:2 + n_res]
    o_ref = refs[2 + n_res]
    if nk == 1:
        acc = jnp.dot(a_ref[...], w_ref[...], preferred_element_type=F32)
        for r in res_refs:
            acc = acc + r[...]
        o_ref[...] = acc.astype(o_ref.dtype)
        return
    acc_ref = refs[3 + n_res]
    k = pl.program_id(2)

    @pl.when(k == 0)
    def _():
        acc_ref[...] = jnp.zeros_like(acc_ref)

    acc_ref[...] += jnp.dot(a_ref[...], w_ref[...], preferred_element_type=F32)

    @pl.when(k == nk - 1)
    def _():
        acc = acc_ref[...]
        for r in res_refs:
            acc = acc + r[...]
        o_ref[...] = acc.astype(o_ref.dtype)


def matmul(a, w, *, residuals=(), out_dtype=BF16, tm=1024, tn=1024, tk=4096):
    M, K = a.shape
    N = w.shape[1]
    tm, tn, tk = _tile(M, tm), _tile(N, tn), _tile(K, tk)
    nk = K // tk
    n_res = len(residuals)
    in_specs = [pl.BlockSpec((tm, tk), lambda i, j, k: (i, k)),
                pl.BlockSpec((tk, tn), lambda i, j, k: (k, j))]
    in_specs += [pl.BlockSpec((tm, tn), lambda i, j, k: (i, j))] * n_res
    osz = jnp.dtype(out_dtype).itemsize
    est = 2 * (tm * tk * 2 + tk * tn * 2 + tm * tn * (osz + 4 * n_res)) + 2 * tm * tn * 4
    return pl.pallas_call(
        functools.partial(_mm_kernel, nk=nk, n_res=n_res),
        grid=(M // tm, N // tn, nk),
        in_specs=in_specs,
        out_specs=pl.BlockSpec((tm, tn), lambda i, j, k: (i, j)),
        out_shape=jax.ShapeDtypeStruct((M, N), out_dtype),
        scratch_shapes=[pltpu.VMEM((tm, tn), F32)] if nk > 1 else [],
        compiler_params=pltpu.CompilerParams(
            dimension_semantics=("parallel", "parallel", "arbitrary"),
            vmem_limit_bytes=_vmem_limit(est)),
        name="matmul",
    )(a, w, *residuals)


def _ret_kernel(q_ref, k_ref, v_ref, gate_ref, cos_ref, sin_ref, dmat_ref, xi_ref, zeta_ref,
                gch_ref, hg_ref, o_ref, r_ref, *, n_chunks, chunk, dk):
    @pl.when(pl.program_id(2) == 0)
    def _():
        r_ref[...] = jnp.zeros_like(r_ref)

    half = dk // 2
    kscale = dk ** -0.5

    def rot(t, cos, sin):
        t1, t2 = t[:, :half], t[:, half:]
        return jnp.concatenate([t1 * cos - t2 * sin, t1 * sin + t2 * cos], axis=-1)

    def body(c, carry):
        rows = pl.ds(pl.multiple_of(c * chunk, chunk), chunk)
        cos, sin = cos_ref[rows, :], sin_ref[rows, :]
        q = rot(q_ref[rows, :].astype(F32), cos, sin)
        k = rot(k_ref[rows, :].astype(F32), cos, sin) * kscale
        vb = v_ref[rows, :]
        inner = lax.dot_general(q.astype(BF16), k.astype(BF16), (((1,), (1,)), ((), ())),
                                preferred_element_type=F32) * dmat_ref[...]
        r_old = r_ref[...]
        o = (jnp.dot(inner.astype(BF16), vb, preferred_element_type=F32)
             + jnp.dot((q * xi_ref[...]).astype(BF16), r_old.astype(BF16),
                       preferred_element_type=F32))
        kz = (k * zeta_ref[...]).astype(BF16)
        r_ref[...] = r_old * gch_ref[...] + lax.dot_general(
            kz, vb, (((0,), (0,)), ((), ())), preferred_element_type=F32)
        on = o * lax.rsqrt(jnp.mean(o * o, axis=-1, keepdims=True) + NORM_EPS) * hg_ref[...]
        g = gate_ref[rows, :].astype(F32)
        silu = g / (1.0 + jnp.exp(-g))
        o_ref[rows, :] = (silu * on).astype(o_ref.dtype)
        return carry

    lax.fori_loop(0, n_chunks, body, 0)


def retention(proj, head_norm_g, *, batch, seq, rows_per_step=1024):
    T, _ = proj.shape
    H, dv = head_norm_g.shape
    dk = dv // 2
    C = RET_CHUNK
    sb = _tile(seq, rows_per_step)
    ns = seq // sb

    inv = 1.0 / (ROPE_BASE ** (jnp.arange(0, dk, 2, dtype=F32) / dk))
    ang = jnp.arange(seq, dtype=F32)[:, None] * inv[None, :]
    cos, sin = jnp.cos(ang), jnp.sin(ang)
    log_g = jnp.log1p(-jnp.exp2(-5.0 - jnp.arange(H, dtype=F32)))
    i = jnp.arange(C, dtype=F32)
    diff = i[:, None] - i[None, :]
    lower = diff >= 0
    dmat = jnp.where(lower[None], jnp.exp(jnp.where(lower, diff, 0.0)[None] * log_g[:, None, None]), 0.0)
    xi = jnp.exp((i[None, :] + 1.0) * log_g[:, None])[:, :, None]
    zeta = jnp.exp((C - 1.0 - i[None, :]) * log_g[:, None])[:, :, None]
    gch = jnp.broadcast_to(jnp.exp(C * log_g)[:, None, None], (H, 1, dv))
    hg = head_norm_g.reshape(H, 1, dv).astype(F32)

    nqk = H * dk // dk
    row_idx = lambda b, h, s: b * ns + s
    in_specs = [
        pl.BlockSpec((sb, dk), lambda b, h, s: (row_idx(b, h, s), h)),
        pl.BlockSpec((sb, dk), lambda b, h, s: (row_idx(b, h, s), nqk + h)),
        pl.BlockSpec((sb, dv), lambda b, h, s: (row_idx(b, h, s), (2 * H * dk) // dv + h)),
        pl.BlockSpec((sb, dv), lambda b, h, s: (row_idx(b, h, s), (2 * H * dk) // dv + H + h)),
        pl.BlockSpec((sb, dk // 2), lambda b, h, s: (s, 0)),
        pl.BlockSpec((sb, dk // 2), lambda b, h, s: (s, 0)),
        pl.BlockSpec((None, C, C), lambda b, h, s: (h, 0, 0)),
        pl.BlockSpec((None, C, 1), lambda b, h, s: (h, 0, 0)),
        pl.BlockSpec((None, C, 1), lambda b, h, s: (h, 0, 0)),
        pl.BlockSpec((None, 1, dv), lambda b, h, s: (h, 0, 0)),
        pl.BlockSpec((None, 1, dv), lambda b, h, s: (h, 0, 0)),
    ]
    est = 2 * sb * (2 * dk * 2 + 3 * dv * 2 + dk * 4) + 8 * dk * dv * 4 + (1 << 22)
    return pl.pallas_call(
        functools.partial(_ret_kernel, n_chunks=sb // C, chunk=C, dk=dk),
        grid=(batch, H, ns),
        in_specs=in_specs,
        out_specs=pl.BlockSpec((sb, dv), lambda b, h, s: (row_idx(b, h, s), h)),
        out_shape=jax.ShapeDtypeStruct((T, H * dv), BF16),
        scratch_shapes=[pltpu.VMEM((dk, dv), F32)],
        compiler_params=pltpu.CompilerParams(
            dimension_semantics=("parallel", "parallel", "arbitrary"),
            vmem_limit_bytes=_vmem_limit(est)),
        name="retention",
    )(proj, proj, proj, proj, cos, sin, dmat, xi, zeta, gch, hg)


def _peer_pairs(topk):
    return [(r1, r2) for r1 in range(topk) for r2 in range(topk) if (r1 + 1) * (r2 + 1) <= topk]


def _first_max(work, iota, n):
    m = jnp.max(work, axis=0, keepdims=True)
    idx = jnp.min(jnp.where(work == m, iota, n), axis=0, keepdims=True)
    return m, iota == idx


def _route_kernel(q_ref, sk_ref, n1_ref, p1_ref, r2_ref, p2_ref, s_ref, vals_ref, cand_ref, *, topk, lane_chunk):
    n_keys, half = sk_ref.shape[1], sk_ref.shape[2]
    tm = q_ref.shape[0]
    pairs = _peer_pairs(topk)
    n_cand = cand_ref.shape[0]
    iota_k = lax.broadcasted_iota(jnp.int32, (n_keys, lane_chunk), 0)
    iota_c = lax.broadcasted_iota(jnp.int32, (n_cand, lane_chunk), 0)
    for side in range(2):
        s_ref[side] = lax.dot_general(sk_ref[side], q_ref[:, side * half:(side + 1) * half],
                                      (((1,), (1,)), ((), ())), preferred_element_type=F32)
    cand_ref[...] = jnp.full(cand_ref.shape, NEG_INF, F32)
    for lc in range(tm // lane_chunk):
        ls = slice(lc * lane_chunk, (lc + 1) * lane_chunk)
        ranks = []
        for side in range(2):
            work = s_ref[side, :, ls]
            rank = jnp.full((n_keys, lane_chunk), float(topk), F32)
            for r in range(topk):
                m, pick = _first_max(work, iota_k, n_keys)
                vals_ref[side, r:r + 1, ls] = m
                work = jnp.where(pick, NEG_INF, work)
                rank = jnp.where(pick, float(r), rank)
            ranks.append(rank)
        for p, (r1, r2) in enumerate(pairs):
            cand_ref[p:p + 1, ls] = vals_ref[0, r1:r1 + 1, ls] + vals_ref[1, r2:r2 + 1, ls]
        work = cand_ref[:, ls]
        cmax = None
        z = jnp.zeros((1, lane_chunk), F32)
        for r in range(topk):
            m, pick = _first_max(work, iota_c, n_cand)
            if r == 0:
                cmax = m
            z = z + jnp.exp(m - cmax)
            work = jnp.where(pick, NEG_INF, work)
        taken = jnp.where((work == NEG_INF) & (iota_c < len(pairs)), 1.0, 0.0)
        n1 = jnp.zeros((n_keys, lane_chunk), F32)
        start = 0
        for r1 in range(topk):
            cnt = topk // (r1 + 1)
            n_r1 = jnp.sum(taken[start:start + cnt, :], axis=0, keepdims=True)
            n1 = jnp.where(ranks[0] == float(r1), n_r1, n1)
            start += cnt
        inv_z = 1.0 / z
        n1_ref[:, ls] = n1
        p1_ref[:, ls] = jnp.where(ranks[0] < float(topk),
                                  jnp.exp(s_ref[0, :, ls] - vals_ref[0, 0:1, ls]) * inv_z, 0.0)
        r2_ref[:, ls] = ranks[1].astype(BF16)
        p2_ref[:, ls] = jnp.where(ranks[1] < float(topk),
                                  jnp.exp(s_ref[1, :, ls] - vals_ref[1, 0:1, ls]), 0.0).astype(BF16)


def peer_route(q, subkeys, *, tm=512, lane_chunk=256):
    T = q.shape[0]
    _, n_keys, half = subkeys.shape
    n_heads = q.shape[1] // (2 * half)
    tm = _tile(T, tm)
    lane_chunk = _tile(tm, lane_chunk)
    n_cand = -(-len(_peer_pairs(PEER_TOPK)) // 8) * 8
    tab = pl.BlockSpec((None, n_keys, tm), lambda i, h: (h, 0, i))
    tab_shape = jax.ShapeDtypeStruct((n_heads, n_keys, T), F32)
    tab_bf = jax.ShapeDtypeStruct((n_heads, n_keys, T), BF16)
    est = 2 * (tm * 2 * half * 2 + 4 * n_keys * tm * 4) + (2 * n_keys + 2 * PEER_TOPK + n_cand) * tm * 4 + (1 << 23)
    return pl.pallas_call(
        functools.partial(_route_kernel, topk=PEER_TOPK, lane_chunk=lane_chunk),
        grid=(T // tm, n_heads),
        in_specs=[pl.BlockSpec((tm, 2 * half), lambda i, h: (i, h)),
                  pl.BlockSpec(subkeys.shape, lambda i, h: (0, 0, 0))],
        out_specs=[tab] * 4,
        out_shape=[tab_shape, tab_shape, tab_bf, tab_bf],
        scratch_shapes=[pltpu.VMEM((2, n_keys, tm), F32), pltpu.VMEM((2, PEER_TOPK, tm), F32),
                        pltpu.VMEM((n_cand, tm), F32)],
        compiler_params=pltpu.CompilerParams(
            dimension_semantics=("parallel", "parallel"), vmem_limit_bytes=_vmem_limit(est)),
        name="peer_route",
    )(q, subkeys)


def _expert_kernel(xn_ref, res_ref, u_ref, v_ref, n1_ref, p1_ref, r2_ref, p2_ref, o_ref, act_ref, g_ref, w0_ref,
                   w1_ref,
                   *, n_e, n_keys, n_i, lane_chunk, k_chunk, d_chunk):
    j = pl.program_id(1)
    tm, D = xn_ref.shape
    n_heads = n1_ref.shape[0]

    @pl.when(j == 0)
    def _():
        o_ref[...] = res_ref[...]
        w1_ref[...] = jnp.zeros_like(w1_ref)

    def gate_rows(il):
        i_row = jnp.minimum(j, n_e - 1) * n_i + il
        sub = 16
        for lc in range(tm // lane_chunk):
            ls = slice(lc * lane_chunk, (lc + 1) * lane_chunk)
            gate = jnp.zeros((n_keys // sub, sub, lane_chunk), BF16)
            for h in range(n_heads):
                n1b = jnp.broadcast_to(n1_ref[h, pl.ds(i_row, 1), :][:, ls], (sub, lane_chunk)).astype(BF16)
                p1b = jnp.broadcast_to(p1_ref[h, pl.ds(i_row, 1), :][:, ls], (sub, lane_chunk)).astype(BF16)
                r2 = r2_ref[h, :, ls].reshape(n_keys // sub, sub, lane_chunk)
                p2 = p2_ref[h, :, ls].reshape(n_keys // sub, sub, lane_chunk)
                gate = gate + jnp.where(r2 < n1b[None], p2 * p1b[None], jnp.zeros_like(p2))
            g_ref[il * n_keys:(il + 1) * n_keys, ls] = gate.reshape(n_keys, lane_chunk)

    @pl.when(j < n_e)
    def _():
        n_k = D // k_chunk
        acc = None
        for step in range(max(n_k, n_i)):
            if step < n_k:
                ks = slice(step * k_chunk, (step + 1) * k_chunk)
                part = lax.dot_general(u_ref[:, ks], xn_ref[:, ks], (((1,), (1,)), ((), ())),
                                       preferred_element_type=F32)
                acc = part if acc is None else acc + part
            if step < n_i:
                gate_rows(step)
        act_ref[...] = acc

    def mix(w_prev_ref, w_cur_ref):
        w_prev = w_prev_ref[...]
        n_d = D // d_chunk
        for step in range(max(n_d, n_i)):
            if step < n_d:
                ds_ = slice(step * d_chunk, (step + 1) * d_chunk)
                o_ref[:, ds_] += lax.dot_general(w_prev, v_ref[:, ds_], (((0,), (0,)), ((), ())),
                                                 preferred_element_type=F32)
            if step < n_i:
                rows_e = slice(step * n_keys, (step + 1) * n_keys)
                a = act_ref[rows_e, :]
                gelu = 0.5 * a * (1.0 + lax.erf(a * INV_SQRT2))
                w_cur_ref[rows_e, :] = g_ref[rows_e, :] * gelu.astype(BF16)

    parity = lax.rem(j, 2)
    pl.when(parity == 0)(lambda: mix(w1_ref, w0_ref))
    pl.when(parity == 1)(lambda: mix(w0_ref, w1_ref))


def peer_experts(xn, res, u, v, n1, p1, r2, p2, *, tm=512, te=512, lane_chunk=256, k_chunk=1024, d_chunk=1024):
    T, D = xn.shape
    E = u.shape[0]
    n_heads, n_keys, _ = n1.shape
    tm, te = _tile(T, tm), _tile(E, te)
    assert te % n_keys == 0
    n_e = E // te
    lane_chunk, k_chunk, d_chunk = _tile(tm, lane_chunk), _tile(D, k_chunk), _tile(D, d_chunk)
    once = dict(pipeline_mode=pl.Buffered(1))
    tab = pl.BlockSpec((n_heads, n_keys, tm), lambda i, j: (0, 0, i), **once)
    est = (tm * D * 2 + tm * D * 4 + 2 * 2 * te * D * 2 + 3 * n_heads * n_keys * tm * 4 + 2 * tm * D * 4
           + te * tm * 10 + 2 * tm * d_chunk * 4 + (1 << 22))
    return pl.pallas_call(
        functools.partial(_expert_kernel, n_e=n_e, n_keys=n_keys, n_i=te // n_keys,
                          lane_chunk=lane_chunk, k_chunk=k_chunk, d_chunk=d_chunk),
        grid=(T // tm, n_e + 1),
        in_specs=[pl.BlockSpec((tm, D), lambda i, j: (i, 0), **once),
                  pl.BlockSpec((tm, D), lambda i, j: (i, 0), **once),
                  pl.BlockSpec((te, D), lambda i, j: (jnp.minimum(j, n_e - 1), 0)),
                  pl.BlockSpec((te, D), lambda i, j: (jnp.maximum(j - 1, 0), 0)),
                  tab, tab, tab, tab],
        out_specs=pl.BlockSpec((tm, D), lambda i, j: (i, 0)),
        out_shape=jax.ShapeDtypeStruct((T, D), F32),
        scratch_shapes=[pltpu.VMEM((te, tm), F32), pltpu.VMEM((te, tm), BF16),
                        pltpu.VMEM((te, tm), BF16), pltpu.VMEM((te, tm), BF16)],
        compiler_params=pltpu.CompilerParams(
            dimension_semantics=("parallel", "arbitrary"), vmem_limit_bytes=_vmem_limit(est)),
        name="peer_experts",
    )(xn, res, u, v, n1, p1, r2, p2)


def peer_ffn(xn, res, w_q, subkeys, u, v):
    q = matmul(xn, w_q)
    return peer_experts(xn, res, u, v, *peer_route(q, subkeys))


def _head_norm(t, g, scale):
    t = t.astype(F32)
    return t * lax.rsqrt(jnp.mean(t * t, axis=-1, keepdims=True) + NORM_EPS) * (g * scale)


def _sb_kernel(q_ref, k_ref, v_ref, qg_ref, kg_ref, o_ref, kn_ref, acc_ref, carry_ref, *, tq, tk, unroll):
    qi = pl.program_id(2)
    dh = q_ref.shape[1]
    seq = k_ref.shape[0]

    @pl.when(qi == 0)
    def _():
        def norm_rows(c, carry):
            rows = pl.ds(pl.multiple_of(c * tq, tq), tq)
            kn_ref[rows, :] = _head_norm(k_ref[rows, :], kg_ref[...], 1.0).astype(BF16)
            return carry
        lax.fori_loop(0, seq // tq, norm_rows, 0)

    qn = _head_norm(q_ref[...], qg_ref[...], dh ** -0.5 * LOG2E).astype(BF16)
    acc_ref[...] = jnp.zeros_like(acc_ref)
    carry_ref[...] = jnp.zeros_like(carry_ref)
    jj = lax.broadcasted_iota(jnp.int32, (tk, tk), 0)
    ss = lax.broadcasted_iota(jnp.int32, (tk, tk), 1)
    upper = (jj > ss).astype(BF16)

    def scores(kb, row0):
        rows = pl.ds(pl.multiple_of(kb * tk, tk), tk)
        z = lax.dot_general(qn[row0:, :], kn_ref[rows, :], (((1,), (1,)), ((), ())),
                            preferred_element_type=F32)
        nz = -z
        l1m = jnp.minimum(nz, 0.0) - jnp.log2(1.0 + jnp.exp2(jnp.minimum(z, nz)))
        causal = None
        if row0 is not None:
            t_idx = qi * tq + row0 + lax.broadcasted_iota(jnp.int32, (tq - row0, tk), 0)
            s_idx = kb * tk + lax.broadcasted_iota(jnp.int32, (tq - row0, tk), 1)
            causal = s_idx < t_idx
            l1m = jnp.where(causal, l1m, 0.0)
        after = jnp.dot(l1m.astype(BF16), upper, preferred_element_type=F32)
        return rows, z + l1m + after, after[:, 0:1] + l1m[:, 0:1], causal, row0 or 0

    def accumulate(rows, logit, total, causal, row0):
        carry = carry_ref[row0:, :]
        a = jnp.exp2(logit + carry)
        if causal is not None:
            a = jnp.where(causal, a, 0.0)
        acc_ref[row0:, :] += jnp.dot(a.astype(BF16), v_ref[rows, :], preferred_element_type=F32)
        carry_ref[row0:, :] = carry + total

    n_diag = tq // tk
    for t in [scores(qi * n_diag + c, c * tk) for c in range(n_diag - 1, -1, -1)]:
        accumulate(*t)

    def loop_body(it, c):
        first = qi * n_diag - 1 - it * unroll
        terms = [scores(first - u, None) for u in range(unroll)]
        for t in terms:
            accumulate(*t)
        return c

    lax.fori_loop(0, (qi * n_diag) // unroll, loop_body, 0)
    o_ref[...] = acc_ref[...].astype(o_ref.dtype)


def sb_attention(q, kv, q_norm_g, k_norm_g, *, batch, seq, tq=1024, tk=256, unroll=4):
    T, HD = q.shape
    dh = q_norm_g.shape[-1]
    H = HD // dh
    tq = _tile(seq, tq)
    tk = _tile(tq, tk)
    nq = seq // tq
    unroll = math.gcd(unroll, tq // tk)
    qg = q_norm_g.reshape(1, dh).astype(F32)
    kg = k_norm_g.reshape(1, dh).astype(F32)
    est = 4 * seq * dh * 2 + seq * dh * 2 + 16 * tq * tk * 4 + (1 << 22)
    return pl.pallas_call(
        functools.partial(_sb_kernel, tq=tq, tk=tk, unroll=unroll),
        grid=(batch, H, nq),
        in_specs=[pl.BlockSpec((tq, dh), lambda b, h, i: (b * nq + i, h)),
                  pl.BlockSpec((seq, dh), lambda b, h, i: (b, h)),
                  pl.BlockSpec((seq, dh), lambda b, h, i: (b, H + h)),
                  pl.BlockSpec((1, dh), lambda b, h, i: (0, 0)),
                  pl.BlockSpec((1, dh), lambda b, h, i: (0, 0))],
        out_specs=pl.BlockSpec((tq, dh), lambda b, h, i: (b * nq + i, h)),
        out_shape=jax.ShapeDtypeStruct((T, HD), BF16),
        scratch_shapes=[pltpu.VMEM((seq, dh), BF16), pltpu.VMEM((tq, dh), F32),
                        pltpu.VMEM((tq, 1), F32)],
        compiler_params=pltpu.CompilerParams(
            dimension_semantics=("parallel", "parallel", "arbitrary"),
            vmem_limit_bytes=_vmem_limit(est)),
        name="sb_attention",
    )(q, kv, kv, qg, kg)


def kernel(x, attn_norm_g, ret_w_in, ret_head_norm_g, ret_w_out, kv_norm_g, w_kv, k_norm_g, sb_w_q, q_norm_g,
           sb_w_out, ffn_norm_g, peer_w_q, peer_subkeys, peer_u, peer_v):
    B, S, D = x.shape
    T = B * S
    depth = attn_norm_g.shape[0]
    n_a = ret_w_in.shape[0]
    h = x.reshape(T, D)
    for layer in range(depth):
        if layer < n_a:
            (xn,) = rms_norms(h, attn_norm_g[layer:layer + 1])
            proj = matmul(xn, to_bf16(ret_w_in, layer))
            o = retention(proj, ret_head_norm_g[layer], batch=B, seq=S)
            h = matmul(o, to_bf16(ret_w_out, layer), residuals=(h,), out_dtype=F32, tk=2048)
        else:
            b = layer - n_a
            if layer == n_a:
                kvn, xn = rms_norms(h, jnp.stack([kv_norm_g, attn_norm_g[layer]]))
                kv = matmul(kvn, to_bf16(w_kv))
            else:
                (xn,) = rms_norms(h, attn_norm_g[layer:layer + 1])
            q = matmul(xn, to_bf16(sb_w_q, b))
            o = sb_attention(q, kv, q_norm_g[b], k_norm_g, batch=B, seq=S)
            h = matmul(o, to_bf16(sb_w_out, b), residuals=(h,), out_dtype=F32, tk=2048)
        (fn,) = rms_norms(h, ffn_norm_g[layer:layer + 1])
        h = peer_ffn(fn, h, to_bf16(peer_w_q, layer), peer_subkeys[layer].astype(BF16),
                     to_bf16(peer_u, layer), to_bf16(peer_v, layer))
    return h.reshape(B, S, D)
```

```python
import functools
import math

import jax
import jax.numpy as jnp
from jax import lax
from jax.experimental import pallas as pl
from jax.experimental.pallas import tpu as pltpu

F32 = jnp.float32
BF16 = jnp.bfloat16

NORM_EPS = 1e-6
ROPE_BASE = 10000.0
RET_CHUNK = 256
PEER_TOPK = 16
V7X_VMEM_BYTES = 64 * 1024 * 1024
VMEM_RESERVE_BYTES = 6 * 1024 * 1024
NEG_INF = float("-inf")
INV_SQRT2 = 1.0 / math.sqrt(2.0)
LOG2E = 1.0 / math.log(2.0)


def _vmem_limit(estimate_bytes):
    return int(min(max(estimate_bytes, 16 * 1024 * 1024), V7X_VMEM_BYTES - VMEM_RESERVE_BYTES))


def _tile(n, pref):
    t = min(n, pref)
    assert n % t == 0, (n, pref)
    return t


def _norm_kernel(x_ref, g_ref, *out_refs):
    x = x_ref[...]
    y = x * lax.rsqrt(jnp.mean(x * x, axis=-1, keepdims=True) + NORM_EPS)
    for gi, o_ref in enumerate(out_refs):
        o_ref[...] = (y * g_ref[gi:gi + 1, :]).astype(o_ref.dtype)


def rms_norms(x, gains, *, tm=256):
    T, D = x.shape
    n = gains.shape[0]
    tm = _tile(T, tm)
    row = pl.BlockSpec((tm, D), lambda i: (i, 0))
    est = 2 * tm * D * (4 + 2 * n) + 4 * tm * D * 4
    return pl.pallas_call(
        _norm_kernel,
        grid=(T // tm,),
        in_specs=[row, pl.BlockSpec((n, D), lambda i: (0, 0))],
        out_specs=[row] * n,
        out_shape=[jax.ShapeDtypeStruct((T, D), BF16)] * n,
        compiler_params=pltpu.CompilerParams(
            dimension_semantics=("parallel",), vmem_limit_bytes=_vmem_limit(est)),
        name="rms_norms",
    )(x, gains)


def _cast_kernel(x_ref, o_ref):
    o_ref[...] = x_ref[...].astype(o_ref.dtype)


def to_bf16(w, index=None, *, tr=512, tc=4096):
    R, C = w.shape[-2:]
    tr, tc = _tile(R, tr), _tile(C, tc)
    if index is None:
        in_spec = pl.BlockSpec((tr, tc), lambda i, j: (i, j))
    else:
        in_spec = pl.BlockSpec((None, tr, tc), lambda i, j: (index, i, j))
    return pl.pallas_call(
        _cast_kernel,
        grid=(R // tr, C // tc),
        in_specs=[in_spec],
        out_specs=pl.BlockSpec((tr, tc), lambda i, j: (i, j)),
        out_shape=jax.ShapeDtypeStruct((R, C), BF16),
        compiler_params=pltpu.CompilerParams(
            dimension_semantics=("parallel", "parallel"), vmem_limit_bytes=_vmem_limit(2 * tr * tc * 6 + (1 << 22))),
        name="to_bf16",
    )(w)


def _mm_kernel(*refs, nk, n_res):
    a_ref, w_ref = refs[0], refs[1]
    res_refs = refs[2:2 + n_res]
    o_ref = refs[2 + n_res]
    if nk == 1:
        acc = jnp.dot(a_ref[...], w_ref[...], preferred_element_type=F32)
        for r in res_refs:
            acc = acc + r[...]
        o_ref[...] = acc.astype(o_ref.dtype)
        return
    acc_ref = refs[3 + n_res]
    k = pl.program_id(2)

    @pl.when(k == 0)
    def _():
        acc_ref[...] = jnp.zeros_like(acc_ref)

    acc_ref[...] += jnp.dot(a_ref[...], w_ref[...], preferred_element_type=F32)

    @pl.when(k == nk - 1)
    def _():
        acc = acc_ref[...]
        for r in res_refs:
            acc = acc + r[...]
        o_ref[...] = acc.astype(o_ref.dtype)


def matmul(a, w, *, residuals=(), out_dtype=BF16, tm=1024, tn=1024, tk=4096):
    M, K = a.shape
    N = w.shape[1]
    tm, tn, tk = _tile(M, tm), _tile(N, tn), _tile(K, tk)
    nk = K // tk
    n_res = len(residuals)
    in_specs = [pl.BlockSpec((tm, tk), lambda i, j, k: (i, k)),
                pl.BlockSpec((tk, tn), lambda i, j, k: (k, j))]
    in_specs += [pl.BlockSpec((tm, tn), lambda i, j, k: (i, j))] * n_res
    osz = jnp.dtype(out_dtype).itemsize
    est = 2 * (tm * tk * 2 + tk * tn * 2 + tm * tn * (osz + 4 * n_res)) + 2 * tm * tn * 4
    return pl.pallas_call(
        functools.partial(_mm_kernel, nk=nk, n_res=n_res),
        grid=(M // tm, N // tn, nk),
        in_specs=in_specs,
        out_specs=pl.BlockSpec((tm, tn), lambda i, j, k: (i, j)),
        out_shape=jax.ShapeDtypeStruct((M, N), out_dtype),
        scratch_shapes=[pltpu.VMEM((tm, tn), F32)] if nk > 1 else [],
        compiler_params=pltpu.CompilerParams(
            dimension_semantics=("parallel", "parallel", "arbitrary"),
            vmem_limit_bytes=_vmem_limit(est)),
        name="matmul",
    )(a, w, *residuals)


def _ret_kernel(q_ref, k_ref, v_ref, gate_ref, cos_ref, sin_ref, dmat_ref, xi_ref, zeta_ref,
                gch_ref, hg_ref, o_ref, r_ref, *, n_chunks, chunk, dk):
    @pl.when(pl.program_id(2) == 0)
    def _():
        r_ref[...] = jnp.zeros_like(r_ref)

    half = dk // 2
    kscale = dk ** -0.5

    def rot(t, cos, sin):
        t1, t2 = t[:, :half], t[:, half:]
        return jnp.concatenate([t1 * cos - t2 * sin, t1 * sin + t2 * cos], axis=-1)

    def body(c):
        rows = pl.ds(c * chunk, chunk)
        cos, sin = cos_ref[rows, :], sin_ref[rows, :]
        q = rot(q_ref[rows, :].astype(F32), cos, sin)
        k = rot(k_ref[rows, :].astype(F32), cos, sin) * kscale
        vb = v_ref[rows, :]
        inner = lax.dot_general(q.astype(BF16), k.astype(BF16), (((1,), (1,)), ((), ())),
                                preferred_element_type=F32) * dmat_ref[...]
        r_old = r_ref[...]
        o = (jnp.dot(inner.astype(BF16), vb, preferred_element_type=F32)
             + jnp.dot((q * xi_ref[...]).astype(BF16), r_old.astype(BF16),
                       preferred_element_type=F32))
        kz = (k * zeta_ref[...]).astype(BF16)
        r_ref[...] = r_old * gch_ref[...] + lax.dot_general(
            kz, vb, (((0,), (0,)), ((), ())), preferred_element_type=F32)
        on = o * lax.rsqrt(jnp.mean(o * o, axis=-1, keepdims=True) + NORM_EPS) * hg_ref[...]
        g = gate_ref[rows, :].astype(F32)
        silu = g / (1.0 + jnp.exp(-g))
        o_ref[rows, :] = (silu * on).astype(o_ref.dtype)

    for c in range(n_chunks):
        body(c)


def retention(proj, head_norm_g, *, batch, seq, rows_per_step=1024):
    T, _ = proj.shape
    H, dv = head_norm_g.shape
    dk = dv // 2
    C = RET_CHUNK
    sb = _tile(seq, rows_per_step)
    ns = seq // sb

    inv = 1.0 / (ROPE_BASE ** (jnp.arange(0, dk, 2, dtype=F32) / dk))
    ang = jnp.arange(seq, dtype=F32)[:, None] * inv[None, :]
    cos, sin = jnp.cos(ang), jnp.sin(ang)
    log_g = jnp.log1p(-jnp.exp2(-5.0 - jnp.arange(H, dtype=F32)))
    i = jnp.arange(C, dtype=F32)
    diff = i[:, None] - i[None, :]
    lower = diff >= 0
    dmat = jnp.where(lower[None], jnp.exp(jnp.where(lower, diff, 0.0)[None] * log_g[:, None, None]), 0.0)
    xi = jnp.exp((i[None, :] + 1.0) * log_g[:, None])[:, :, None]
    zeta = jnp.exp((C - 1.0 - i[None, :]) * log_g[:, None])[:, :, None]
    gch = jnp.broadcast_to(jnp.exp(C * log_g)[:, None, None], (H, 1, dv))
    hg = head_norm_g.reshape(H, 1, dv).astype(F32)

    nqk = H * dk // dk
    row_idx = lambda b, h, s: b * ns + s
    in_specs = [
        pl.BlockSpec((sb, dk), lambda b, h, s: (row_idx(b, h, s), h)),
        pl.BlockSpec((sb, dk), lambda b, h, s: (row_idx(b, h, s), nqk + h)),
        pl.BlockSpec((sb, dv), lambda b, h, s: (row_idx(b, h, s), (2 * H * dk) // dv + h)),
        pl.BlockSpec((sb, dv), lambda b, h, s: (row_idx(b, h, s), (2 * H * dk) // dv + H + h)),
        pl.BlockSpec((sb, dk // 2), lambda b, h, s: (s, 0)),
        pl.BlockSpec((sb, dk // 2), lambda b, h, s: (s, 0)),
        pl.BlockSpec((None, C, C), lambda b, h, s: (h, 0, 0)),
        pl.BlockSpec((None, C, 1), lambda b, h, s: (h, 0, 0)),
        pl.BlockSpec((None, C, 1), lambda b, h, s: (h, 0, 0)),
        pl.BlockSpec((None, 1, dv), lambda b, h, s: (h, 0, 0)),
        pl.BlockSpec((None, 1, dv), lambda b, h, s: (h, 0, 0)),
    ]
    est = 2 * sb * (2 * dk * 2 + 3 * dv * 2 + dk * 4) + 8 * dk * dv * 4 + (1 << 22)
    return pl.pallas_call(
        functools.partial(_ret_kernel, n_chunks=sb // C, chunk=C, dk=dk),
        grid=(batch, H, ns),
        in_specs=in_specs,
        out_specs=pl.BlockSpec((sb, dv), lambda b, h, s: (row_idx(b, h, s), h)),
        out_shape=jax.ShapeDtypeStruct((T, H * dv), BF16),
        scratch_shapes=[pltpu.VMEM((dk, dv), F32)],
        compiler_params=pltpu.CompilerParams(
            dimension_semantics=("parallel", "parallel", "arbitrary"),
            vmem_limit_bytes=_vmem_limit(est)),
        name="retention",
    )(proj, proj, proj, proj, cos, sin, dmat, xi, zeta, gch, hg)


def _peer_pairs(topk):
    return [(r1, r2) for r1 in range(topk) for r2 in range(topk) if (r1 + 1) * (r2 + 1) <= topk]


def _first_max(work, iota, n):
    m = jnp.max(work, axis=0, keepdims=True)
    idx = jnp.min(jnp.where(work == m, iota, n), axis=0, keepdims=True)
    return m, iota == idx


def _route_kernel(q_ref, sk_ref, n1_ref, p1_ref, r2_ref, p2_ref, s_ref, vals_ref, cand_ref, *, topk, lane_chunk):
    n_keys, half = sk_ref.shape[1], sk_ref.shape[2]
    tm = q_ref.shape[0]
    pairs = _peer_pairs(topk)
    n_cand = cand_ref.shape[0]
    iota_k = lax.broadcasted_iota(jnp.int32, (n_keys, lane_chunk), 0)
    iota_c = lax.broadcasted_iota(jnp.int32, (n_cand, lane_chunk), 0)
    for side in range(2):
        s_ref[side] = lax.dot_general(sk_ref[side], q_ref[:, side * half:(side + 1) * half],
                                      (((1,), (1,)), ((), ())), preferred_element_type=F32)
    cand_ref[...] = jnp.full(cand_ref.shape, NEG_INF, F32)
    for lc in range(tm // lane_chunk):
        ls = slice(lc * lane_chunk, (lc + 1) * lane_chunk)
        ranks = []
        for side in range(2):
            work = s_ref[side, :, ls]
            rank = jnp.full((n_keys, lane_chunk), float(topk), F32)
            for r in range(topk):
                m, pick = _first_max(work, iota_k, n_keys)
                vals_ref[side, r:r + 1, ls] = m
                work = jnp.where(pick, NEG_INF, work)
                rank = jnp.where(pick, float(r), rank)
            ranks.append(rank)
        for p, (r1, r2) in enumerate(pairs):
            cand_ref[p:p + 1, ls] = vals_ref[0, r1:r1 + 1, ls] + vals_ref[1, r2:r2 + 1, ls]
        work = cand_ref[:, ls]
        cmax = None
        z = jnp.zeros((1, lane_chunk), F32)
        for r in range(topk):
            m, pick = _first_max(work, iota_c, n_cand)
            if r == 0:
                cmax = m
            z = z + jnp.exp(m - cmax)
            work = jnp.where(pick, NEG_INF, work)
        taken = jnp.where((work == NEG_INF) & (iota_c < len(pairs)), 1.0, 0.0)
        n1 = jnp.zeros((n_keys, lane_chunk), F32)
        start = 0
        for r1 in range(topk):
            cnt = topk // (r1 + 1)
            n_r1 = jnp.sum(taken[start:start + cnt, :], axis=0, keepdims=True)
            n1 = jnp.where(ranks[0] == float(r1), n_r1, n1)
            start += cnt
        inv_z = 1.0 / z
        n1_ref[:, ls] = n1
        p1_ref[:, ls] = jnp.where(ranks[0] < float(topk),
                                  jnp.exp(s_ref[0, :, ls] - vals_ref[0, 0:1, ls]) * inv_z, 0.0)
        r2_ref[:, ls] = ranks[1].astype(BF16)
        p2_ref[:, ls] = jnp.where(ranks[1] < float(topk),
                                  jnp.exp(s_ref[1, :, ls] - vals_ref[1, 0:1, ls]), 0.0).astype(BF16)


def peer_route(q, subkeys, *, tm=512, lane_chunk=256):
    T = q.shape[0]
    _, n_keys, half = subkeys.shape
    n_heads = q.shape[1] // (2 * half)
    tm = _tile(T, tm)
    lane_chunk = _tile(tm, lane_chunk)
    n_cand = -(-len(_peer_pairs(PEER_TOPK)) // 8) * 8
    tab = pl.BlockSpec((None, n_keys, tm), lambda i, h: (h, 0, i))
    tab_shape = jax.ShapeDtypeStruct((n_heads, n_keys, T), F32)
    tab_bf = jax.ShapeDtypeStruct((n_heads, n_keys, T), BF16)
    est = 2 * (tm * 2 * half * 2 + 4 * n_keys * tm * 4) + (2 * n_keys + 2 * PEER_TOPK + n_cand) * tm * 4 + (1 << 23)
    return pl.pallas_call(
        functools.partial(_route_kernel, topk=PEER_TOPK, lane_chunk=lane_chunk),
        grid=(T // tm, n_heads),
        in_specs=[pl.BlockSpec((tm, 2 * half), lambda i, h: (i, h)),
                  pl.BlockSpec(subkeys.shape, lambda i, h: (0, 0, 0))],
        out_specs=[tab] * 4,
        out_shape=[tab_shape, tab_shape, tab_bf, tab_bf],
        scratch_shapes=[pltpu.VMEM((2, n_keys, tm), F32), pltpu.VMEM((2, PEER_TOPK, tm), F32),
                        pltpu.VMEM((n_cand, tm), F32)],
        compiler_params=pltpu.CompilerParams(
            dimension_semantics=("parallel", "parallel"), vmem_limit_bytes=_vmem_limit(est)),
        name="peer_route",
    )(q, subkeys)


def _expert_kernel(xn_ref, res_ref, u_ref, v_ref, n1_ref, p1_ref, r2_ref, p2_ref, o_ref, w_ref, *, n_keys, n_i):
    j = pl.program_id(1)
    tm = xn_ref.shape[0]
    n_heads = n1_ref.shape[0]
    sub = 16

    @pl.when(j == 0)
    def _():
        o_ref[...] = res_ref[...]

    def gates(il):
        i_row = j * n_i + il
        gate = jnp.zeros((n_keys // sub, sub, tm), BF16)
        for h in range(n_heads):
            n1b = jnp.broadcast_to(n1_ref[h, pl.ds(i_row, 1), :], (sub, tm)).astype(BF16)
            p1b = jnp.broadcast_to(p1_ref[h, pl.ds(i_row, 1), :], (sub, tm)).astype(BF16)
            r2 = r2_ref[h].reshape(n_keys // sub, sub, tm)
            p2 = p2_ref[h].reshape(n_keys // sub, sub, tm)
            gate = gate + jnp.where(r2 < n1b[None], p2 * p1b[None], jnp.zeros_like(p2))
        return gate.reshape(n_keys, tm)

    act = lax.dot_general(u_ref[...], xn_ref[...], (((1,), (1,)), ((), ())), preferred_element_type=F32)
    for il in range(n_i):
        rows_e = slice(il * n_keys, (il + 1) * n_keys)
        a = act[rows_e, :]
        gelu = 0.5 * a * (1.0 + lax.erf(a * INV_SQRT2))
        w_ref[rows_e, :] = gates(il) * gelu.astype(BF16)
    o_ref[...] += lax.dot_general(w_ref[...], v_ref[...], (((0,), (0,)), ((), ())), preferred_element_type=F32)


def peer_experts(xn, res, u, v, n1, p1, r2, p2, *, tm=512, te=512):
    T, D = xn.shape
    E = u.shape[0]
    n_heads, n_keys, _ = n1.shape
    tm, te = _tile(T, tm), _tile(E, te)
    assert te % n_keys == 0
    once = dict(pipeline_mode=pl.Buffered(1))
    tab = pl.BlockSpec((n_heads, n_keys, tm), lambda i, j: (0, 0, i), **once)
    est = (tm * D * 2 + tm * D * 4 + 2 * 2 * te * D * 2 + 3 * n_heads * n_keys * tm * 4 + 2 * tm * D * 4
           + te * tm * 8 + tm * D * 4 + (1 << 22))
    return pl.pallas_call(
        functools.partial(_expert_kernel, n_keys=n_keys, n_i=te // n_keys),
        grid=(T // tm, E // te),
        in_specs=[pl.BlockSpec((tm, D), lambda i, j: (i, 0), **once),
                  pl.BlockSpec((tm, D), lambda i, j: (i, 0), **once),
                  pl.BlockSpec((te, D), lambda i, j: (j, 0)),
                  pl.BlockSpec((te, D), lambda i, j: (j, 0)),
                  tab, tab, tab, tab],
        out_specs=pl.BlockSpec((tm, D), lambda i, j: (i, 0)),
        out_shape=jax.ShapeDtypeStruct((T, D), F32),
        scratch_shapes=[pltpu.VMEM((te, tm), BF16)],
        compiler_params=pltpu.CompilerParams(
            dimension_semantics=("parallel", "arbitrary"), vmem_limit_bytes=_vmem_limit(est)),
        name="peer_experts",
    )(xn, res, u, v, n1, p1, r2, p2)


def peer_ffn(xn, res, w_q, subkeys, u, v):
    q = matmul(xn, w_q)
    return peer_experts(xn, res, u, v, *peer_route(q, subkeys))


def _head_norm(t, g, scale):
    t = t.astype(F32)
    return t * lax.rsqrt(jnp.mean(t * t, axis=-1, keepdims=True) + NORM_EPS) * (g * scale)


def _sb_kernel(q_ref, k_ref, v_ref, qg_ref, kg_ref, o_ref, kn_ref, acc_ref, carry_ref, *, tq, tk, unroll):
    qi = pl.program_id(2)
    dh = q_ref.shape[1]
    seq = k_ref.shape[0]

    @pl.when(qi == 0)
    def _():
        def norm_rows(c, carry):
            rows = pl.ds(pl.multiple_of(c * tq, tq), tq)
            kn_ref[rows, :] = _head_norm(k_ref[rows, :], kg_ref[...], 1.0).astype(BF16)
            return carry
        lax.fori_loop(0, seq // tq, norm_rows, 0)

    qn = _head_norm(q_ref[...], qg_ref[...], dh ** -0.5 * LOG2E).astype(BF16)
    acc_ref[...] = jnp.zeros_like(acc_ref)
    carry_ref[...] = jnp.zeros_like(carry_ref)
    jj = lax.broadcasted_iota(jnp.int32, (tk, tk), 0)
    ss = lax.broadcasted_iota(jnp.int32, (tk, tk), 1)
    upper = (jj > ss).astype(BF16)

    def scores(kb, row0):
        rows = pl.ds(pl.multiple_of(kb * tk, tk), tk)
        z = lax.dot_general(qn[row0:, :], kn_ref[rows, :], (((1,), (1,)), ((), ())),
                            preferred_element_type=F32)
        nz = -z
        l1m = jnp.minimum(nz, 0.0) - jnp.log2(1.0 + jnp.exp2(jnp.minimum(z, nz)))
        causal = None
        if row0 is not None:
            t_idx = qi * tq + row0 + lax.broadcasted_iota(jnp.int32, (tq - row0, tk), 0)
            s_idx = kb * tk + lax.broadcasted_iota(jnp.int32, (tq - row0, tk), 1)
            causal = s_idx < t_idx
            l1m = jnp.where(causal, l1m, 0.0)
        after = jnp.dot(l1m.astype(BF16), upper, preferred_element_type=F32)
        return rows, z + l1m + after, after[:, 0:1] + l1m[:, 0:1], causal, row0 or 0

    def accumulate(rows, logit, total, causal, row0):
        carry = carry_ref[row0:, :]
        a = jnp.exp2(logit + carry)
        if causal is not None:
            a = jnp.where(causal, a, 0.0)
        acc_ref[row0:, :] += jnp.dot(a.astype(BF16), v_ref[rows, :], preferred_element_type=F32)
        carry_ref[row0:, :] = carry + total

    n_diag = tq // tk
    for t in [scores(qi * n_diag + c, c * tk) for c in range(n_diag - 1, -1, -1)]:
        accumulate(*t)

    def loop_body(it, c):
        first = qi * n_diag - 1 - it * unroll
        terms = [scores(first - u, None) for u in range(unroll)]
        for t in terms:
            accumulate(*t)
        return c

    lax.fori_loop(0, (qi * n_diag) // unroll, loop_body, 0)
    o_ref[...] = acc_ref[...].astype(o_ref.dtype)


def sb_attention(q, kv, q_norm_g, k_norm_g, *, batch, seq, tq=1024, tk=256, unroll=4):
    T, HD = q.shape
    dh = q_norm_g.shape[-1]
    H = HD // dh
    tq = _tile(seq, tq)
    tk = _tile(tq, tk)
    nq = seq // tq
    unroll = math.gcd(unroll, tq // tk)
    qg = q_norm_g.reshape(1, dh).astype(F32)
    kg = k_norm_g.reshape(1, dh).astype(F32)
    est = 4 * seq * dh * 2 + seq * dh * 2 + 16 * tq * tk * 4 + (1 << 22)
    return pl.pallas_call(
        functools.partial(_sb_kernel, tq=tq, tk=tk, unroll=unroll),
        grid=(batch, H, nq),
        in_specs=[pl.BlockSpec((tq, dh), lambda b, h, i: (b * nq + i, h)),
                  pl.BlockSpec((seq, dh), lambda b, h, i: (b, h)),
                  pl.BlockSpec((seq, dh), lambda b, h, i: (b, H + h)),
                  pl.BlockSpec((1, dh), lambda b, h, i: (0, 0)),
                  pl.BlockSpec((1, dh), lambda b, h, i: (0, 0))],
        out_specs=pl.BlockSpec((tq, dh), lambda b, h, i: (b * nq + i, h)),
        out_shape=jax.ShapeDtypeStruct((T, HD), BF16),
        scratch_shapes=[pltpu.VMEM((seq, dh), BF16), pltpu.VMEM((tq, dh), F32),
                        pltpu.VMEM((tq, 1), F32)],
        compiler_params=pltpu.CompilerParams(
            dimension_semantics=("parallel", "parallel", "arbitrary"),
            vmem_limit_bytes=_vmem_limit(est)),
        name="sb_attention",
    )(q, kv, kv, qg, kg)


def kernel(x, attn_norm_g, ret_w_in, ret_head_norm_g, ret_w_out, kv_norm_g, w_kv, k_norm_g, sb_w_q, q_norm_g,
           sb_w_out, ffn_norm_g, peer_w_q, peer_subkeys, peer_u, peer_v):
    B, S, D = x.shape
    T = B * S
    depth = attn_norm_g.shape[0]
    n_a = ret_w_in.shape[0]
    h = x.reshape(T, D)
    for layer in range(depth):
        if layer < n_a:
            (xn,) = rms_norms(h, attn_norm_g[layer:layer + 1])
            proj = matmul(xn, to_bf16(ret_w_in, layer))
            o = retention(proj, ret_head_norm_g[layer], batch=B, seq=S)
            h = matmul(o, to_bf16(ret_w_out, layer), residuals=(h,), out_dtype=F32, tk=2048)
        else:
            b = layer - n_a
            if layer == n_a:
                kvn, xn = rms_norms(h, jnp.stack([kv_norm_g, attn_norm_g[layer]]))
                kv = matmul(kvn, to_bf16(w_kv))
            else:
                (xn,) = rms_norms(h, attn_norm_g[layer:layer + 1])
            q = matmul(xn, to_bf16(sb_w_q, b))
            o = sb_attention(q, kv, q_norm_g[b], k_norm_g, batch=B, seq=S)
            h = matmul(o, to_bf16(sb_w_out, b), residuals=(h,), out_dtype=F32, tk=2048)
        (fn,) = rms_norms(h, ffn_norm_g[layer:layer + 1])
        h = peer_ffn(fn, h, to_bf16(peer_w_q, layer), peer_subkeys[layer].astype(BF16),
                     to_bf16(peer_u, layer), to_bf16(peer_v, layer))
    return h.reshape(B, S, D)
```

```python
import functools
import math

import jax
import jax.numpy as jnp
from jax import lax
from jax.experimental import pallas as pl
from jax.experimental.pallas import tpu as pltpu

F32 = jnp.float32
BF16 = jnp.bfloat16

NORM_EPS = 1e-6
ROPE_BASE = 10000.0
RET_CHUNK = 256
PEER_TOPK = 16
V7X_VMEM_BYTES = 64 * 1024 * 1024
VMEM_RESERVE_BYTES = 6 * 1024 * 1024
NEG_INF = float("-inf")
INV_SQRT2 = 1.0 / math.sqrt(2.0)
LOG2E = 1.0 / math.log(2.0)
SB_DEAD_LOG2 = -160.0


def _vmem_limit(estimate_bytes):
    return int(min(max(estimate_bytes, 16 * 1024 * 1024), V7X_VMEM_BYTES - VMEM_RESERVE_BYTES))


def _tile(n, pref):
    t = min(n, pref)
    assert n % t == 0, (n, pref)
    return t


def _norm_kernel(x_ref, g_ref, *out_refs):
    x = x_ref[...]
    y = x * lax.rsqrt(jnp.mean(x * x, axis=-1, keepdims=True) + NORM_EPS)
    for gi, o_ref in enumerate(out_refs):
        o_ref[...] = (y * g_ref[gi:gi + 1, :]).astype(o_ref.dtype)


def rms_norms(x, gains, *, tm=256):
    T, D = x.shape
    n = gains.shape[0]
    tm = _tile(T, tm)
    row = pl.BlockSpec((tm, D), lambda i: (i, 0))
    est = 2 * tm * D * (4 + 2 * n) + 4 * tm * D * 4
    return pl.pallas_call(
        _norm_kernel,
        grid=(T // tm,),
        in_specs=[row, pl.BlockSpec((n, D), lambda i: (0, 0))],
        out_specs=[row] * n,
        out_shape=[jax.ShapeDtypeStruct((T, D), BF16)] * n,
        compiler_params=pltpu.CompilerParams(
            dimension_semantics=("parallel",), vmem_limit_bytes=_vmem_limit(est)),
        name="rms_norms",
    )(x, gains)


def _cast_kernel(x_ref, o_ref):
    o_ref[...] = x_ref[...].astype(o_ref.dtype)


def to_bf16(w, index=None, *, tr=512, tc=4096):
    R, C = w.shape[-2:]
    tr, tc = _tile(R, tr), _tile(C, tc)
    if index is None:
        in_spec = pl.BlockSpec((tr, tc), lambda i, j: (i, j))
    else:
        in_spec = pl.BlockSpec((None, tr, tc), lambda i, j: (index, i, j))
    return pl.pallas_call(
        _cast_kernel,
        grid=(R // tr, C // tc),
        in_specs=[in_spec],
        out_specs=pl.BlockSpec((tr, tc), lambda i, j: (i, j)),
        out_shape=jax.ShapeDtypeStruct((R, C), BF16),
        compiler_params=pltpu.CompilerParams(
            dimension_semantics=("parallel", "parallel"), vmem_limit_bytes=_vmem_limit(2 * tr * tc * 6 + (1 << 22))),
        name="to_bf16",
    )(w)


def _mm_kernel(*refs, nk, n_res):
    a_ref, w_ref = refs[0], refs[1]
    res_refs = refs[2:2 + n_res]
    o_ref = refs[2 + n_res]
    if nk == 1:
        acc = jnp.dot(a_ref[...], w_ref[...], preferred_element_type=F32)
        for r in res_refs:
            acc = acc + r[...]
        o_ref[...] = acc.astype(o_ref.dtype)
        return
    acc_ref = refs[3 + n_res]
    k = pl.program_id(2)

    @pl.when(k == 0)
    def _():
        acc_ref[...] = jnp.zeros_like(acc_ref)

    acc_ref[...] += jnp.dot(a_ref[...], w_ref[...], preferred_element_type=F32)

    @pl.when(k == nk - 1)
    def _():
        acc = acc_ref[...]
        for r in res_refs:
            acc = acc + r[...]
        o_ref[...] = acc.astype(o_ref.dtype)


def matmul(a, w, *, residuals=(), out_dtype=BF16, tm=1024, tn=1024, tk=4096):
    M, K = a.shape
    N = w.shape[1]
    tm, tn, tk = _tile(M, tm), _tile(N, tn), _tile(K, tk)
    nk = K // tk
    n_res = len(residuals)
    in_specs = [pl.BlockSpec((tm, tk), lambda i, j, k: (i, k)),
                pl.BlockSpec((tk, tn), lambda i, j, k: (k, j))]
    in_specs += [pl.BlockSpec((tm, tn), lambda i, j, k: (i, j))] * n_res
    osz = jnp.dtype(out_dtype).itemsize
    est = 2 * (tm * tk * 2 + tk * tn * 2 + tm * tn * (osz + 4 * n_res)) + 2 * tm * tn * 4
    return pl.pallas_call(
        functools.partial(_mm_kernel, nk=nk, n_res=n_res),
        grid=(M // tm, N // tn, nk),
        in_specs=in_specs,
        out_specs=pl.BlockSpec((tm, tn), lambda i, j, k: (i, j)),
        out_shape=jax.ShapeDtypeStruct((M, N), out_dtype),
        scratch_shapes=[pltpu.VMEM((tm, tn), F32)] if nk > 1 else [],
        compiler_params=pltpu.CompilerParams(
            dimension_semantics=("parallel", "parallel", "arbitrary"),
            vmem_limit_bytes=_vmem_limit(est)),
        name="matmul",
    )(a, w, *residuals)


def _ret_kernel(q_ref, k_ref, v_ref, gate_ref, cos_ref, sin_ref, dmat_ref, xi_ref, zeta_ref,
                gch_ref, hg_ref, o_ref, r_ref, *, n_chunks, chunk, dk):
    @pl.when(pl.program_id(2) == 0)
    def _():
        r_ref[...] = jnp.zeros_like(r_ref)

    half = dk // 2
    kscale = dk ** -0.5

    def rot(t, cos, sin):
        t1, t2 = t[:, :half], t[:, half:]
        return jnp.concatenate([t1 * cos - t2 * sin, t1 * sin + t2 * cos], axis=-1)

    def body(c):
        rows = pl.ds(c * chunk, chunk)
        cos, sin = cos_ref[rows, :], sin_ref[rows, :]
        q = rot(q_ref[rows, :].astype(F32), cos, sin)
        k = rot(k_ref[rows, :].astype(F32), cos, sin) * kscale
        vb = v_ref[rows, :]
        inner = lax.dot_general(q.astype(BF16), k.astype(BF16), (((1,), (1,)), ((), ())),
                                preferred_element_type=F32) * dmat_ref[...]
        r_old = r_ref[...]
        o = (jnp.dot(inner.astype(BF16), vb, preferred_element_type=F32)
             + jnp.dot((q * xi_ref[...]).astype(BF16), r_old.astype(BF16),
                       preferred_element_type=F32))
        kz = (k * zeta_ref[...]).astype(BF16)
        r_ref[...] = r_old * gch_ref[...] + lax.dot_general(
            kz, vb, (((0,), (0,)), ((), ())), preferred_element_type=F32)
        on = o * lax.rsqrt(jnp.mean(o * o, axis=-1, keepdims=True) + NORM_EPS) * hg_ref[...]
        g = gate_ref[rows, :].astype(F32)
        silu = g / (1.0 + jnp.exp(-g))
        o_ref[rows, :] = (silu * on).astype(o_ref.dtype)

    for c in range(n_chunks):
        body(c)


def retention(proj, head_norm_g, *, batch, seq, rows_per_step=1024):
    T, _ = proj.shape
    H, dv = head_norm_g.shape
    dk = dv // 2
    C = RET_CHUNK
    sb = _tile(seq, rows_per_step)
    ns = seq // sb

    inv = 1.0 / (ROPE_BASE ** (jnp.arange(0, dk, 2, dtype=F32) / dk))
    ang = jnp.arange(seq, dtype=F32)[:, None] * inv[None, :]
    cos, sin = jnp.cos(ang), jnp.sin(ang)
    log_g = jnp.log1p(-jnp.exp2(-5.0 - jnp.arange(H, dtype=F32)))
    i = jnp.arange(C, dtype=F32)
    diff = i[:, None] - i[None, :]
    lower = diff >= 0
    dmat = jnp.where(lower[None], jnp.exp(jnp.where(lower, diff, 0.0)[None] * log_g[:, None, None]), 0.0)
    xi = jnp.exp((i[None, :] + 1.0) * log_g[:, None])[:, :, None]
    zeta = jnp.exp((C - 1.0 - i[None, :]) * log_g[:, None])[:, :, None]
    gch = jnp.broadcast_to(jnp.exp(C * log_g)[:, None, None], (H, 1, dv))
    hg = head_norm_g.reshape(H, 1, dv).astype(F32)

    nqk = H * dk // dk
    row_idx = lambda b, h, s: b * ns + s
    in_specs = [
        pl.BlockSpec((sb, dk), lambda b, h, s: (row_idx(b, h, s), h)),
        pl.BlockSpec((sb, dk), lambda b, h, s: (row_idx(b, h, s), nqk + h)),
        pl.BlockSpec((sb, dv), lambda b, h, s: (row_idx(b, h, s), (2 * H * dk) // dv + h)),
        pl.BlockSpec((sb, dv), lambda b, h, s: (row_idx(b, h, s), (2 * H * dk) // dv + H + h)),
        pl.BlockSpec((sb, dk // 2), lambda b, h, s: (s, 0)),
        pl.BlockSpec((sb, dk // 2), lambda b, h, s: (s, 0)),
        pl.BlockSpec((None, C, C), lambda b, h, s: (h, 0, 0)),
        pl.BlockSpec((None, C, 1), lambda b, h, s: (h, 0, 0)),
        pl.BlockSpec((None, C, 1), lambda b, h, s: (h, 0, 0)),
        pl.BlockSpec((None, 1, dv), lambda b, h, s: (h, 0, 0)),
        pl.BlockSpec((None, 1, dv), lambda b, h, s: (h, 0, 0)),
    ]
    est = 2 * sb * (2 * dk * 2 + 3 * dv * 2 + dk * 4) + 8 * dk * dv * 4 + (1 << 22)
    return pl.pallas_call(
        functools.partial(_ret_kernel, n_chunks=sb // C, chunk=C, dk=dk),
        grid=(batch, H, ns),
        in_specs=in_specs,
        out_specs=pl.BlockSpec((sb, dv), lambda b, h, s: (row_idx(b, h, s), h)),
        out_shape=jax.ShapeDtypeStruct((T, H * dv), BF16),
        scratch_shapes=[pltpu.VMEM((dk, dv), F32)],
        compiler_params=pltpu.CompilerParams(
            dimension_semantics=("parallel", "parallel", "arbitrary"),
            vmem_limit_bytes=_vmem_limit(est)),
        name="retention",
    )(proj, proj, proj, proj, cos, sin, dmat, xi, zeta, gch, hg)


def _peer_pairs(topk):
    return [(r1, r2) for r1 in range(topk) for r2 in range(topk) if (r1 + 1) * (r2 + 1) <= topk]


def _first_max(work, iota, n):
    m = jnp.max(work, axis=0, keepdims=True)
    idx = jnp.min(jnp.where(work == m, iota, n), axis=0, keepdims=True)
    return m, iota == idx


def _route_kernel(q_ref, sk_ref, n1_ref, p1_ref, r2_ref, p2_ref, s_ref, vals_ref, cand_ref, *, topk, lane_chunk):
    n_keys, half = sk_ref.shape[1], sk_ref.shape[2]
    tm = q_ref.shape[0]
    pairs = _peer_pairs(topk)
    n_cand = cand_ref.shape[0]
    iota_k = lax.broadcasted_iota(jnp.int32, (n_keys, lane_chunk), 0)
    iota_c = lax.broadcasted_iota(jnp.int32, (n_cand, lane_chunk), 0)
    for side in range(2):
        s_ref[side] = lax.dot_general(sk_ref[side], q_ref[:, side * half:(side + 1) * half],
                                      (((1,), (1,)), ((), ())), preferred_element_type=F32)
    cand_ref[...] = jnp.full(cand_ref.shape, NEG_INF, F32)
    for lc in range(tm // lane_chunk):
        ls = slice(lc * lane_chunk, (lc + 1) * lane_chunk)
        ranks = []
        for side in range(2):
            work = s_ref[side, :, ls]
            rank = jnp.full((n_keys, lane_chunk), float(topk), F32)
            for r in range(topk):
                m, pick = _first_max(work, iota_k, n_keys)
                vals_ref[side, r:r + 1, ls] = m
                work = jnp.where(pick, NEG_INF, work)
                rank = jnp.where(pick, float(r), rank)
            ranks.append(rank)
        for p, (r1, r2) in enumerate(pairs):
            cand_ref[p:p + 1, ls] = vals_ref[0, r1:r1 + 1, ls] + vals_ref[1, r2:r2 + 1, ls]
        work = cand_ref[:, ls]
        cmax = None
        z = jnp.zeros((1, lane_chunk), F32)
        for r in range(topk):
            m, pick = _first_max(work, iota_c, n_cand)
            if r == 0:
                cmax = m
            z = z + jnp.exp(m - cmax)
            work = jnp.where(pick, NEG_INF, work)
        taken = jnp.where((work == NEG_INF) & (iota_c < len(pairs)), 1.0, 0.0)
        n1 = jnp.zeros((n_keys, lane_chunk), F32)
        start = 0
        for r1 in range(topk):
            cnt = topk // (r1 + 1)
            n_r1 = jnp.sum(taken[start:start + cnt, :], axis=0, keepdims=True)
            n1 = jnp.where(ranks[0] == float(r1), n_r1, n1)
            start += cnt
        inv_z = 1.0 / z
        n1_ref[:, ls] = n1
        p1_ref[:, ls] = jnp.where(ranks[0] < float(topk),
                                  jnp.exp(s_ref[0, :, ls] - vals_ref[0, 0:1, ls]) * inv_z, 0.0)
        r2_ref[:, ls] = ranks[1].astype(BF16)
        p2_ref[:, ls] = jnp.where(ranks[1] < float(topk),
                                  jnp.exp(s_ref[1, :, ls] - vals_ref[1, 0:1, ls]), 0.0).astype(BF16)


def peer_route(q, subkeys, *, tm=512, lane_chunk=256):
    T = q.shape[0]
    _, n_keys, half = subkeys.shape
    n_heads = q.shape[1] // (2 * half)
    tm = _tile(T, tm)
    lane_chunk = _tile(tm, lane_chunk)
    n_cand = -(-len(_peer_pairs(PEER_TOPK)) // 8) * 8
    tab = pl.BlockSpec((None, n_keys, tm), lambda i, h: (h, 0, i))
    tab_shape = jax.ShapeDtypeStruct((n_heads, n_keys, T), F32)
    tab_bf = jax.ShapeDtypeStruct((n_heads, n_keys, T), BF16)
    est = 2 * (tm * 2 * half * 2 + 4 * n_keys * tm * 4) + (2 * n_keys + 2 * PEER_TOPK + n_cand) * tm * 4 + (1 << 23)
    return pl.pallas_call(
        functools.partial(_route_kernel, topk=PEER_TOPK, lane_chunk=lane_chunk),
        grid=(T // tm, n_heads),
        in_specs=[pl.BlockSpec((tm, 2 * half), lambda i, h: (i, h)),
                  pl.BlockSpec(subkeys.shape, lambda i, h: (0, 0, 0))],
        out_specs=[tab] * 4,
        out_shape=[tab_shape, tab_shape, tab_bf, tab_bf],
        scratch_shapes=[pltpu.VMEM((2, n_keys, tm), F32), pltpu.VMEM((2, PEER_TOPK, tm), F32),
                        pltpu.VMEM((n_cand, tm), F32)],
        compiler_params=pltpu.CompilerParams(
            dimension_semantics=("parallel", "parallel"), vmem_limit_bytes=_vmem_limit(est)),
        name="peer_route",
    )(q, subkeys)


def _expert_kernel(xn_ref, res_ref, u_ref, v_ref, n1_ref, p1_ref, r2_ref, p2_ref, o_ref, w_ref, *, n_keys, n_i):
    j = pl.program_id(1)
    tm = xn_ref.shape[0]
    n_heads = n1_ref.shape[0]
    sub = 16

    @pl.when(j == 0)
    def _():
        o_ref[...] = res_ref[...]

    def gates(il):
        i_row = j * n_i + il
        gate = jnp.zeros((n_keys // sub, sub, tm), BF16)
        for h in range(n_heads):
            n1b = jnp.broadcast_to(n1_ref[h, pl.ds(i_row, 1), :], (sub, tm)).astype(BF16)
            p1b = jnp.broadcast_to(p1_ref[h, pl.ds(i_row, 1), :], (sub, tm)).astype(BF16)
            r2 = r2_ref[h].reshape(n_keys // sub, sub, tm)
            p2 = p2_ref[h].reshape(n_keys // sub, sub, tm)
            gate = gate + jnp.where(r2 < n1b[None], p2 * p1b[None], jnp.zeros_like(p2))
        return gate.reshape(n_keys, tm)

    act = lax.dot_general(u_ref[...], xn_ref[...], (((1,), (1,)), ((), ())), preferred_element_type=F32)
    for il in range(n_i):
        rows_e = slice(il * n_keys, (il + 1) * n_keys)
        a = act[rows_e, :]
        gelu = 0.5 * a * (1.0 + lax.erf(a * INV_SQRT2))
        w_ref[rows_e, :] = gates(il) * gelu.astype(BF16)
    o_ref[...] += lax.dot_general(w_ref[...], v_ref[...], (((0,), (0,)), ((), ())), preferred_element_type=F32)


def peer_experts(xn, res, u, v, n1, p1, r2, p2, *, tm=512, te=512):
    T, D = xn.shape
    E = u.shape[0]
    n_heads, n_keys, _ = n1.shape
    tm, te = _tile(T, tm), _tile(E, te)
    assert te % n_keys == 0
    once = dict(pipeline_mode=pl.Buffered(1))
    tab = pl.BlockSpec((n_heads, n_keys, tm), lambda i, j: (0, 0, i), **once)
    est = (tm * D * 2 + tm * D * 4 + 2 * 2 * te * D * 2 + 3 * n_heads * n_keys * tm * 4 + 2 * tm * D * 4
           + te * tm * 8 + tm * D * 4 + (1 << 22))
    return pl.pallas_call(
        functools.partial(_expert_kernel, n_keys=n_keys, n_i=te // n_keys),
        grid=(T // tm, E // te),
        in_specs=[pl.BlockSpec((tm, D), lambda i, j: (i, 0), **once),
                  pl.BlockSpec((tm, D), lambda i, j: (i, 0), **once),
                  pl.BlockSpec((te, D), lambda i, j: (j, 0)),
                  pl.BlockSpec((te, D), lambda i, j: (j, 0)),
                  tab, tab, tab, tab],
        out_specs=pl.BlockSpec((tm, D), lambda i, j: (i, 0)),
        out_shape=jax.ShapeDtypeStruct((T, D), F32),
        scratch_shapes=[pltpu.VMEM((te, tm), BF16)],
        compiler_params=pltpu.CompilerParams(
            dimension_semantics=("parallel", "arbitrary"), vmem_limit_bytes=_vmem_limit(est)),
        name="peer_experts",
    )(xn, res, u, v, n1, p1, r2, p2)


def peer_ffn(xn, res, w_q, subkeys, u, v):
    q = matmul(xn, w_q)
    return peer_experts(xn, res, u, v, *peer_route(q, subkeys))


def _head_norm(t, g, scale):
    t = t.astype(F32)
    return t * lax.rsqrt(jnp.mean(t * t, axis=-1, keepdims=True) + NORM_EPS) * (g * scale)


def _sb_kernel(q_ref, k_ref, v_ref, qg_ref, kg_ref, o_ref, kn_ref, acc_ref, carry_ref, *, tq, tk, unroll):
    qi = pl.program_id(2)
    dh = q_ref.shape[1]
    seq = k_ref.shape[0]

    @pl.when(qi == 0)
    def _():
        def norm_rows(c, carry):
            rows = pl.ds(pl.multiple_of(c * tq, tq), tq)
            kn_ref[rows, :] = _head_norm(k_ref[rows, :], kg_ref[...], 1.0).astype(BF16)
            return carry
        lax.fori_loop(0, seq // tq, norm_rows, 0)

    qn = _head_norm(q_ref[...], qg_ref[...], dh ** -0.5 * LOG2E).astype(BF16)
    acc_ref[...] = jnp.zeros_like(acc_ref)
    carry_ref[...] = jnp.zeros_like(carry_ref)
    jj = lax.broadcasted_iota(jnp.int32, (tk, tk), 0)
    ss = lax.broadcasted_iota(jnp.int32, (tk, tk), 1)
    upper = (jj > ss).astype(BF16)

    def scores(kb, row0):
        rows = pl.ds(pl.multiple_of(kb * tk, tk), tk)
        z = lax.dot_general(qn[row0:, :], kn_ref[rows, :], (((1,), (1,)), ((), ())),
                            preferred_element_type=F32)
        nz = -z
        l1m = jnp.minimum(nz, 0.0) - jnp.log2(1.0 + jnp.exp2(jnp.minimum(z, nz)))
        causal = None
        if row0 is not None:
            t_idx = qi * tq + row0 + lax.broadcasted_iota(jnp.int32, (tq - row0, tk), 0)
            s_idx = kb * tk + lax.broadcasted_iota(jnp.int32, (tq - row0, tk), 1)
            causal = s_idx < t_idx
            l1m = jnp.where(causal, l1m, 0.0)
        after = jnp.dot(l1m.astype(BF16), upper, preferred_element_type=F32)
        return rows, z + l1m + after, after[:, 0:1] + l1m[:, 0:1], causal, row0 or 0

    def accumulate(rows, logit, total, causal, row0):
        carry = carry_ref[row0:, :]
        a = jnp.exp2(logit + carry)
        if causal is not None:
            a = jnp.where(causal, a, 0.0)
        acc_ref[row0:, :] += jnp.dot(a.astype(BF16), v_ref[rows, :], preferred_element_type=F32)
        carry_ref[row0:, :] = carry + total

    n_diag = tq // tk
    for t in [scores(qi * n_diag + c, c * tk) for c in range(n_diag - 1, -1, -1)]:
        accumulate(*t)

    n_groups = (qi * n_diag) // unroll

    def alive():
        return jnp.max(carry_ref[...]) > SB_DEAD_LOG2

    def group(state):
        it, _ = state
        first = qi * n_diag - 1 - it * unroll
        terms = [scores(first - u, None) for u in range(unroll)]
        for t in terms:
            accumulate(*t)
        return it + 1, alive()

    lax.while_loop(lambda st: jnp.logical_and(st[0] < n_groups, st[1]), group, (jnp.int32(0), alive()))
    o_ref[...] = acc_ref[...].astype(o_ref.dtype)


def sb_attention(q, kv, q_norm_g, k_norm_g, *, batch, seq, tq=1024, tk=256, unroll=1):
    T, HD = q.shape
    dh = q_norm_g.shape[-1]
    H = HD // dh
    tq = _tile(seq, tq)
    tk = _tile(tq, tk)
    nq = seq // tq
    unroll = math.gcd(unroll, tq // tk)
    qg = q_norm_g.reshape(1, dh).astype(F32)
    kg = k_norm_g.reshape(1, dh).astype(F32)
    est = 4 * seq * dh * 2 + seq * dh * 2 + 16 * tq * tk * 4 + (1 << 22)
    return pl.pallas_call(
        functools.partial(_sb_kernel, tq=tq, tk=tk, unroll=unroll),
        grid=(batch, H, nq),
        in_specs=[pl.BlockSpec((tq, dh), lambda b, h, i: (b * nq + i, h)),
                  pl.BlockSpec((seq, dh), lambda b, h, i: (b, h)),
                  pl.BlockSpec((seq, dh), lambda b, h, i: (b, H + h)),
                  pl.BlockSpec((1, dh), lambda b, h, i: (0, 0)),
                  pl.BlockSpec((1, dh), lambda b, h, i: (0, 0))],
        out_specs=pl.BlockSpec((tq, dh), lambda b, h, i: (b * nq + i, h)),
        out_shape=jax.ShapeDtypeStruct((T, HD), BF16),
        scratch_shapes=[pltpu.VMEM((seq, dh), BF16), pltpu.VMEM((tq, dh), F32),
                        pltpu.VMEM((tq, 1), F32)],
        compiler_params=pltpu.CompilerParams(
            dimension_semantics=("parallel", "parallel", "arbitrary"),
            vmem_limit_bytes=_vmem_limit(est)),
        name="sb_attention",
    )(q, kv, kv, qg, kg)


def kernel(x, attn_norm_g, ret_w_in, ret_head_norm_g, ret_w_out, kv_norm_g, w_kv, k_norm_g, sb_w_q, q_norm_g,
           sb_w_out, ffn_norm_g, peer_w_q, peer_subkeys, peer_u, peer_v):
    B, S, D = x.shape
    T = B * S
    depth = attn_norm_g.shape[0]
    n_a = ret_w_in.shape[0]
    h = x.reshape(T, D)
    for layer in range(depth):
        if layer < n_a:
            (xn,) = rms_norms(h, attn_norm_g[layer:layer + 1])
            proj = matmul(xn, to_bf16(ret_w_in, layer))
            o = retention(proj, ret_head_norm_g[layer], batch=B, seq=S)
            h = matmul(o, to_bf16(ret_w_out, layer), residuals=(h,), out_dtype=F32, tk=2048)
        else:
            b = layer - n_a
            if layer == n_a:
                kvn, xn = rms_norms(h, jnp.stack([kv_norm_g, attn_norm_g[layer]]))
                kv = matmul(kvn, to_bf16(w_kv))
            else:
                (xn,) = rms_norms(h, attn_norm_g[layer:layer + 1])
            q = matmul(xn, to_bf16(sb_w_q, b))
            o = sb_attention(q, kv, q_norm_g[b], k_norm_g, batch=B, seq=S)
            h = matmul(o, to_bf16(sb_w_out, b), residuals=(h,), out_dtype=F32, tk=2048)
        (fn,) = rms_norms(h, ffn_norm_g[layer:layer + 1])
        h = peer_ffn(fn, h, to_bf16(peer_w_q, layer), peer_subkeys[layer].astype(BF16),
                     to_bf16(peer_u, layer), to_bf16(peer_v, layer))
    return h.reshape(B, S, D)
```

```python
import functools
import math

import jax
import jax.numpy as jnp
from jax import lax
from jax.experimental import pallas as pl
from jax.experimental.pallas import tpu as pltpu

F32 = jnp.float32
BF16 = jnp.bfloat16

NORM_EPS = 1e-6
ROPE_BASE = 10000.0
RET_CHUNK = 256
PEER_TOPK = 16
V7X_VMEM_BYTES = 64 * 1024 * 1024
VMEM_RESERVE_BYTES = 6 * 1024 * 1024
NEG_INF = float("-inf")
INV_SQRT2 = 1.0 / math.sqrt(2.0)
LOG2E = 1.0 / math.log(2.0)
SB_DEAD_LOG2 = -160.0


def _vmem_limit(estimate_bytes):
    return int(min(max(estimate_bytes, 16 * 1024 * 1024), V7X_VMEM_BYTES - VMEM_RESERVE_BYTES))


def _tile(n, pref):
    t = min(n, pref)
    assert n % t == 0, (n, pref)
    return t


def _norm_kernel(x_ref, g_ref, *out_refs):
    x = x_ref[...]
    y = x * lax.rsqrt(jnp.mean(x * x, axis=-1, keepdims=True) + NORM_EPS)
    for gi, o_ref in enumerate(out_refs):
        o_ref[...] = (y * g_ref[gi:gi + 1, :]).astype(o_ref.dtype)


def rms_norms(x, gains, *, tm=256):
    T, D = x.shape
    n = gains.shape[0]
    tm = _tile(T, tm)
    row = pl.BlockSpec((tm, D), lambda i: (i, 0))
    est = 2 * tm * D * (4 + 2 * n) + 4 * tm * D * 4
    return pl.pallas_call(
        _norm_kernel,
        grid=(T // tm,),
        in_specs=[row, pl.BlockSpec((n, D), lambda i: (0, 0))],
        out_specs=[row] * n,
        out_shape=[jax.ShapeDtypeStruct((T, D), BF16)] * n,
        compiler_params=pltpu.CompilerParams(
            dimension_semantics=("parallel",), vmem_limit_bytes=_vmem_limit(est)),
        name="rms_norms",
    )(x, gains)


def _cast_kernel(x_ref, o_ref):
    o_ref[...] = x_ref[...].astype(o_ref.dtype)


def to_bf16(w, index=None, *, tr=512, tc=4096):
    R, C = w.shape[-2:]
    tr, tc = _tile(R, tr), _tile(C, tc)
    if index is None:
        in_spec = pl.BlockSpec((tr, tc), lambda i, j: (i, j))
    else:
        in_spec = pl.BlockSpec((None, tr, tc), lambda i, j: (index, i, j))
    return pl.pallas_call(
        _cast_kernel,
        grid=(R // tr, C // tc),
        in_specs=[in_spec],
        out_specs=pl.BlockSpec((tr, tc), lambda i, j: (i, j)),
        out_shape=jax.ShapeDtypeStruct((R, C), BF16),
        compiler_params=pltpu.CompilerParams(
            dimension_semantics=("parallel", "parallel"), vmem_limit_bytes=_vmem_limit(2 * tr * tc * 6 + (1 << 22))),
        name="to_bf16",
    )(w)


def _mm_kernel(*refs, nk, n_res, n_side):
    a_ref, w_ref = refs[0], refs[1]
    res_refs = refs[2:2 + n_res]
    side_in = refs[2 + n_res:2 + n_res + n_side]
    o_ref = refs[2 + n_res + n_side]
    side_out = refs[3 + n_res + n_side:3 + n_res + 2 * n_side]
    for s_in, s_out in zip(side_in, side_out):
        s_out[...] = s_in[...].astype(s_out.dtype)
    if nk == 1:
        acc = jnp.dot(a_ref[...], w_ref[...], preferred_element_type=F32)
        for r in res_refs:
            acc = acc + r[...]
        o_ref[...] = acc.astype(o_ref.dtype)
        return
    acc_ref = refs[3 + n_res + 2 * n_side]
    k = pl.program_id(2)

    @pl.when(k == 0)
    def _():
        acc_ref[...] = jnp.zeros_like(acc_ref)

    acc_ref[...] += jnp.dot(a_ref[...], w_ref[...], preferred_element_type=F32)

    @pl.when(k == nk - 1)
    def _():
        acc = acc_ref[...]
        for r in res_refs:
            acc = acc + r[...]
        o_ref[...] = acc.astype(o_ref.dtype)


def _side_rows(n_rows, n_steps):
    for rb in range(16, n_rows + 1, 16):
        if n_rows % rb == 0 and rb * n_steps >= n_rows:
            return rb
    return n_rows


def matmul(a, w, *, residuals=(), side=(), out_dtype=BF16, tm=1024, tn=1024, tk=4096):
    M, K = a.shape
    N = w.shape[1]
    tm, tn, tk = _tile(M, tm), _tile(N, tn), _tile(K, tk)
    nk = K // tk
    nj = N // tn
    n_res, n_side = len(residuals), len(side)
    n_steps = (M // tm) * nj * nk
    in_specs = [pl.BlockSpec((tm, tk), lambda i, j, k: (i, k)),
                pl.BlockSpec((tk, tn), lambda i, j, k: (k, j))]
    in_specs += [pl.BlockSpec((tm, tn), lambda i, j, k: (i, j))] * n_res
    out_specs = [pl.BlockSpec((tm, tn), lambda i, j, k: (i, j))]
    out_shape = [jax.ShapeDtypeStruct((M, N), out_dtype)]
    osz = jnp.dtype(out_dtype).itemsize
    est = 2 * (tm * tk * 2 + tk * tn * 2 + tm * tn * (osz + 4 * n_res)) + 2 * tm * tn * 4
    for p, l in side:
        _, R, C = p.shape
        rb = _side_rows(R, n_steps)
        last = R // rb - 1
        blk = lambda i, j, k, last=last: jnp.minimum((i * nj + j) * nk + k, last)
        in_specs.append(pl.BlockSpec((None, rb, C), lambda i, j, k, l=l, blk=blk: (l, blk(i, j, k), 0)))
        out_specs.append(pl.BlockSpec((rb, C), lambda i, j, k, blk=blk: (blk(i, j, k), 0)))
        out_shape.append(jax.ShapeDtypeStruct((R, C), BF16))
        est += 2 * rb * C * 6
    outs = pl.pallas_call(
        functools.partial(_mm_kernel, nk=nk, n_res=n_res, n_side=n_side),
        grid=(M // tm, nj, nk),
        in_specs=in_specs,
        out_specs=out_specs,
        out_shape=out_shape,
        scratch_shapes=[pltpu.VMEM((tm, tn), F32)] if nk > 1 else [],
        compiler_params=pltpu.CompilerParams(
            dimension_semantics=("arbitrary",) * 3 if n_side else ("parallel", "parallel", "arbitrary"),
            vmem_limit_bytes=_vmem_limit(est)),
        name="matmul",
    )(a, w, *residuals, *[p for p, _ in side])
    return outs if n_side else outs[0]


def _ret_kernel(q_ref, k_ref, v_ref, gate_ref, cos_ref, sin_ref, dmat_ref, xi_ref, zeta_ref,
                gch_ref, hg_ref, o_ref, r_ref, *, n_chunks, chunk, dk):
    @pl.when(pl.program_id(2) == 0)
    def _():
        r_ref[...] = jnp.zeros_like(r_ref)

    half = dk // 2
    kscale = dk ** -0.5

    def rot(t, cos, sin):
        t1, t2 = t[:, :half], t[:, half:]
        return jnp.concatenate([t1 * cos - t2 * sin, t1 * sin + t2 * cos], axis=-1)

    def body(c):
        rows = pl.ds(c * chunk, chunk)
        cos, sin = cos_ref[rows, :], sin_ref[rows, :]
        q = rot(q_ref[rows, :].astype(F32), cos, sin)
        k = rot(k_ref[rows, :].astype(F32), cos, sin) * kscale
        vb = v_ref[rows, :]
        inner = lax.dot_general(q.astype(BF16), k.astype(BF16), (((1,), (1,)), ((), ())),
                                preferred_element_type=F32) * dmat_ref[...]
        r_old = r_ref[...]
        o = (jnp.dot(inner.astype(BF16), vb, preferred_element_type=F32)
             + jnp.dot((q * xi_ref[...]).astype(BF16), r_old.astype(BF16),
                       preferred_element_type=F32))
        kz = (k * zeta_ref[...]).astype(BF16)
        r_ref[...] = r_old * gch_ref[...] + lax.dot_general(
            kz, vb, (((0,), (0,)), ((), ())), preferred_element_type=F32)
        on = o * lax.rsqrt(jnp.mean(o * o, axis=-1, keepdims=True) + NORM_EPS) * hg_ref[...]
        g = gate_ref[rows, :].astype(F32)
        silu = g / (1.0 + jnp.exp(-g))
        o_ref[rows, :] = (silu * on).astype(o_ref.dtype)

    for c in range(n_chunks):
        body(c)


def retention(proj, head_norm_g, *, batch, seq, rows_per_step=1024):
    T, _ = proj.shape
    H, dv = head_norm_g.shape
    dk = dv // 2
    C = RET_CHUNK
    sb = _tile(seq, rows_per_step)
    ns = seq // sb

    inv = 1.0 / (ROPE_BASE ** (jnp.arange(0, dk, 2, dtype=F32) / dk))
    ang = jnp.arange(seq, dtype=F32)[:, None] * inv[None, :]
    cos, sin = jnp.cos(ang), jnp.sin(ang)
    log_g = jnp.log1p(-jnp.exp2(-5.0 - jnp.arange(H, dtype=F32)))
    i = jnp.arange(C, dtype=F32)
    diff = i[:, None] - i[None, :]
    lower = diff >= 0
    dmat = jnp.where(lower[None], jnp.exp(jnp.where(lower, diff, 0.0)[None] * log_g[:, None, None]), 0.0)
    xi = jnp.exp((i[None, :] + 1.0) * log_g[:, None])[:, :, None]
    zeta = jnp.exp((C - 1.0 - i[None, :]) * log_g[:, None])[:, :, None]
    gch = jnp.broadcast_to(jnp.exp(C * log_g)[:, None, None], (H, 1, dv))
    hg = head_norm_g.reshape(H, 1, dv).astype(F32)

    nqk = H * dk // dk
    row_idx = lambda b, h, s: b * ns + s
    in_specs = [
        pl.BlockSpec((sb, dk), lambda b, h, s: (row_idx(b, h, s), h)),
        pl.BlockSpec((sb, dk), lambda b, h, s: (row_idx(b, h, s), nqk + h)),
        pl.BlockSpec((sb, dv), lambda b, h, s: (row_idx(b, h, s), (2 * H * dk) // dv + h)),
        pl.BlockSpec((sb, dv), lambda b, h, s: (row_idx(b, h, s), (2 * H * dk) // dv + H + h)),
        pl.BlockSpec((sb, dk // 2), lambda b, h, s: (s, 0)),
        pl.BlockSpec((sb, dk // 2), lambda b, h, s: (s, 0)),
        pl.BlockSpec((None, C, C), lambda b, h, s: (h, 0, 0)),
        pl.BlockSpec((None, C, 1), lambda b, h, s: (h, 0, 0)),
        pl.BlockSpec((None, C, 1), lambda b, h, s: (h, 0, 0)),
        pl.BlockSpec((None, 1, dv), lambda b, h, s: (h, 0, 0)),
        pl.BlockSpec((None, 1, dv), lambda b, h, s: (h, 0, 0)),
    ]
    est = 2 * sb * (2 * dk * 2 + 3 * dv * 2 + dk * 4) + 8 * dk * dv * 4 + (1 << 22)
    return pl.pallas_call(
        functools.partial(_ret_kernel, n_chunks=sb // C, chunk=C, dk=dk),
        grid=(batch, H, ns),
        in_specs=in_specs,
        out_specs=pl.BlockSpec((sb, dv), lambda b, h, s: (row_idx(b, h, s), h)),
        out_shape=jax.ShapeDtypeStruct((T, H * dv), BF16),
        scratch_shapes=[pltpu.VMEM((dk, dv), F32)],
        compiler_params=pltpu.CompilerParams(
            dimension_semantics=("parallel", "parallel", "arbitrary"),
            vmem_limit_bytes=_vmem_limit(est)),
        name="retention",
    )(proj, proj, proj, proj, cos, sin, dmat, xi, zeta, gch, hg)


def _peer_pairs(topk):
    return [(r1, r2) for r1 in range(topk) for r2 in range(topk) if (r1 + 1) * (r2 + 1) <= topk]


def _first_max(work, iota, n):
    m = jnp.max(work, axis=0, keepdims=True)
    idx = jnp.min(jnp.where(work == m, iota, n), axis=0, keepdims=True)
    return m, iota == idx


def _route_kernel(q_ref, sk_ref, n1_ref, p1_ref, r2_ref, p2_ref, s_ref, vals_ref, cand_ref, *, topk, lane_chunk):
    n_keys, half = sk_ref.shape[1], sk_ref.shape[2]
    tm = q_ref.shape[0]
    pairs = _peer_pairs(topk)
    n_cand = cand_ref.shape[0]
    iota_k = lax.broadcasted_iota(jnp.int32, (n_keys, lane_chunk), 0)
    iota_c = lax.broadcasted_iota(jnp.int32, (n_cand, lane_chunk), 0)
    for side in range(2):
        s_ref[side] = lax.dot_general(sk_ref[side], q_ref[:, side * half:(side + 1) * half],
                                      (((1,), (1,)), ((), ())), preferred_element_type=F32)
    cand_ref[...] = jnp.full(cand_ref.shape, NEG_INF, F32)
    for lc in range(tm // lane_chunk):
        ls = slice(lc * lane_chunk, (lc + 1) * lane_chunk)
        ranks = []
        for side in range(2):
            work = s_ref[side, :, ls]
            rank = jnp.full((n_keys, lane_chunk), float(topk), F32)
            for r in range(topk):
                m, pick = _first_max(work, iota_k, n_keys)
                vals_ref[side, r:r + 1, ls] = m
                work = jnp.where(pick, NEG_INF, work)
                rank = jnp.where(pick, float(r), rank)
            ranks.append(rank)
        for p, (r1, r2) in enumerate(pairs):
            cand_ref[p:p + 1, ls] = vals_ref[0, r1:r1 + 1, ls] + vals_ref[1, r2:r2 + 1, ls]
        work = cand_ref[:, ls]
        cmax = None
        z = jnp.zeros((1, lane_chunk), F32)
        for r in range(topk):
            m, pick = _first_max(work, iota_c, n_cand)
            if r == 0:
                cmax = m
            z = z + jnp.exp(m - cmax)
            work = jnp.where(pick, NEG_INF, work)
        taken = jnp.where((work == NEG_INF) & (iota_c < len(pairs)), 1.0, 0.0)
        n1 = jnp.zeros((n_keys, lane_chunk), F32)
        start = 0
        for r1 in range(topk):
            cnt = topk // (r1 + 1)
            n_r1 = jnp.sum(taken[start:start + cnt, :], axis=0, keepdims=True)
            n1 = jnp.where(ranks[0] == float(r1), n_r1, n1)
            start += cnt
        inv_z = 1.0 / z
        n1_ref[:, ls] = n1
        p1_ref[:, ls] = jnp.where(ranks[0] < float(topk),
                                  jnp.exp(s_ref[0, :, ls] - vals_ref[0, 0:1, ls]) * inv_z, 0.0)
        r2_ref[:, ls] = ranks[1].astype(BF16)
        p2_ref[:, ls] = jnp.where(ranks[1] < float(topk),
                                  jnp.exp(s_ref[1, :, ls] - vals_ref[1, 0:1, ls]), 0.0).astype(BF16)


def peer_route(q, subkeys, *, tm=512, lane_chunk=256):
    T = q.shape[0]
    _, n_keys, half = subkeys.shape
    n_heads = q.shape[1] // (2 * half)
    tm = _tile(T, tm)
    lane_chunk = _tile(tm, lane_chunk)
    n_cand = -(-len(_peer_pairs(PEER_TOPK)) // 8) * 8
    tab = pl.BlockSpec((None, n_keys, tm), lambda i, h: (h, 0, i))
    tab_shape = jax.ShapeDtypeStruct((n_heads, n_keys, T), F32)
    tab_bf = jax.ShapeDtypeStruct((n_heads, n_keys, T), BF16)
    est = 2 * (tm * 2 * half * 2 + 4 * n_keys * tm * 4) + (2 * n_keys + 2 * PEER_TOPK + n_cand) * tm * 4 + (1 << 23)
    return pl.pallas_call(
        functools.partial(_route_kernel, topk=PEER_TOPK, lane_chunk=lane_chunk),
        grid=(T // tm, n_heads),
        in_specs=[pl.BlockSpec((tm, 2 * half), lambda i, h: (i, h)),
                  pl.BlockSpec(subkeys.shape, lambda i, h: (0, 0, 0))],
        out_specs=[tab] * 4,
        out_shape=[tab_shape, tab_shape, tab_bf, tab_bf],
        scratch_shapes=[pltpu.VMEM((2, n_keys, tm), F32), pltpu.VMEM((2, PEER_TOPK, tm), F32),
                        pltpu.VMEM((n_cand, tm), F32)],
        compiler_params=pltpu.CompilerParams(
            dimension_semantics=("parallel", "parallel"), vmem_limit_bytes=_vmem_limit(est)),
        name="peer_route",
    )(q, subkeys)


def _expert_kernel(xn_ref, res_ref, u_ref, v_ref, n1_ref, p1_ref, r2_ref, p2_ref, o_ref, w_ref, *, n_keys, n_i):
    j = pl.program_id(1)
    tm = xn_ref.shape[0]
    n_heads = n1_ref.shape[0]
    sub = 16

    @pl.when(j == 0)
    def _():
        o_ref[...] = res_ref[...]

    def gates(il):
        i_row = j * n_i + il
        gate = jnp.zeros((n_keys // sub, sub, tm), BF16)
        for h in range(n_heads):
            n1b = jnp.broadcast_to(n1_ref[h, pl.ds(i_row, 1), :], (sub, tm)).astype(BF16)
            p1b = jnp.broadcast_to(p1_ref[h, pl.ds(i_row, 1), :], (sub, tm)).astype(BF16)
            r2 = r2_ref[h].reshape(n_keys // sub, sub, tm)
            p2 = p2_ref[h].reshape(n_keys // sub, sub, tm)
            gate = gate + jnp.where(r2 < n1b[None], p2 * p1b[None], jnp.zeros_like(p2))
        return gate.reshape(n_keys, tm)

    act = lax.dot_general(u_ref[...], xn_ref[...], (((1,), (1,)), ((), ())), preferred_element_type=F32)
    for il in range(n_i):
        rows_e = slice(il * n_keys, (il + 1) * n_keys)
        a = act[rows_e, :]
        gelu = 0.5 * a * (1.0 + lax.erf(a * INV_SQRT2))
        w_ref[rows_e, :] = gates(il) * gelu.astype(BF16)
    o_ref[...] += lax.dot_general(w_ref[...], v_ref[...], (((0,), (0,)), ((), ())), preferred_element_type=F32)


def peer_experts(xn, res, u, v, n1, p1, r2, p2, *, tm=512, te=512):
    T, D = xn.shape
    E = u.shape[0]
    n_heads, n_keys, _ = n1.shape
    tm, te = _tile(T, tm), _tile(E, te)
    assert te % n_keys == 0
    once = dict(pipeline_mode=pl.Buffered(1))
    tab = pl.BlockSpec((n_heads, n_keys, tm), lambda i, j: (0, 0, i), **once)
    est = (tm * D * 2 + tm * D * 4 + 2 * 2 * te * D * 2 + 3 * n_heads * n_keys * tm * 4 + 2 * tm * D * 4
           + te * tm * 8 + tm * D * 4 + (1 << 22))
    return pl.pallas_call(
        functools.partial(_expert_kernel, n_keys=n_keys, n_i=te // n_keys),
        grid=(T // tm, E // te),
        in_specs=[pl.BlockSpec((tm, D), lambda i, j: (i, 0), **once),
                  pl.BlockSpec((tm, D), lambda i, j: (i, 0), **once),
                  pl.BlockSpec((te, D), lambda i, j: (j, 0)),
                  pl.BlockSpec((te, D), lambda i, j: (j, 0)),
                  tab, tab, tab, tab],
        out_specs=pl.BlockSpec((tm, D), lambda i, j: (i, 0)),
        out_shape=jax.ShapeDtypeStruct((T, D), F32),
        scratch_shapes=[pltpu.VMEM((te, tm), BF16)],
        compiler_params=pltpu.CompilerParams(
            dimension_semantics=("parallel", "arbitrary"), vmem_limit_bytes=_vmem_limit(est)),
        name="peer_experts",
    )(xn, res, u, v, n1, p1, r2, p2)


def peer_ffn(xn, res, w_q, subkeys, u, v):
    q = matmul(xn, w_q)
    return peer_experts(xn, res, u, v, *peer_route(q, subkeys))


def _head_norm(t, g, scale):
    t = t.astype(F32)
    return t * lax.rsqrt(jnp.mean(t * t, axis=-1, keepdims=True) + NORM_EPS) * (g * scale)


def _sb_kernel(q_ref, k_ref, v_ref, qg_ref, kg_ref, o_ref, kn_ref, acc_ref, carry_ref, *, tq, tk, unroll):
    qi = pl.program_id(2)
    dh = q_ref.shape[1]
    seq = k_ref.shape[0]

    @pl.when(qi == 0)
    def _():
        def norm_rows(c, carry):
            rows = pl.ds(pl.multiple_of(c * tq, tq), tq)
            kn_ref[rows, :] = _head_norm(k_ref[rows, :], kg_ref[...], 1.0).astype(BF16)
            return carry
        lax.fori_loop(0, seq // tq, norm_rows, 0)

    qn = _head_norm(q_ref[...], qg_ref[...], dh ** -0.5 * LOG2E).astype(BF16)
    acc_ref[...] = jnp.zeros_like(acc_ref)
    carry_ref[...] = jnp.zeros_like(carry_ref)
    jj = lax.broadcasted_iota(jnp.int32, (tk, tk), 0)
    ss = lax.broadcasted_iota(jnp.int32, (tk, tk), 1)
    upper = (jj > ss).astype(BF16)

    def scores(kb, row0):
        rows = pl.ds(pl.multiple_of(kb * tk, tk), tk)
        z = lax.dot_general(qn[row0:, :], kn_ref[rows, :], (((1,), (1,)), ((), ())),
                            preferred_element_type=F32)
        nz = -z
        l1m = jnp.minimum(nz, 0.0) - jnp.log2(1.0 + jnp.exp2(jnp.minimum(z, nz)))
        causal = None
        if row0 is not None:
            t_idx = qi * tq + row0 + lax.broadcasted_iota(jnp.int32, (tq - row0, tk), 0)
            s_idx = kb * tk + lax.broadcasted_iota(jnp.int32, (tq - row0, tk), 1)
            causal = s_idx < t_idx
            l1m = jnp.where(causal, l1m, 0.0)
        after = jnp.dot(l1m.astype(BF16), upper, preferred_element_type=F32)
        return rows, z + l1m + after, after[:, 0:1] + l1m[:, 0:1], causal, row0 or 0

    def accumulate(rows, logit, total, causal, row0):
        carry = carry_ref[row0:, :]
        a = jnp.exp2(logit + carry)
        if causal is not None:
            a = jnp.where(causal, a, 0.0)
        acc_ref[row0:, :] += jnp.dot(a.astype(BF16), v_ref[rows, :], preferred_element_type=F32)
        carry_ref[row0:, :] = carry + total

    n_diag = tq // tk
    for t in [scores(qi * n_diag + c, c * tk) for c in range(n_diag - 1, -1, -1)]:
        accumulate(*t)

    n_groups = (qi * n_diag) // unroll

    def alive():
        return jnp.max(carry_ref[...]) > SB_DEAD_LOG2

    def group(state):
        it, _ = state
        first = qi * n_diag - 1 - it * unroll
        terms = [scores(first - u, None) for u in range(unroll)]
        for t in terms:
            accumulate(*t)
        return it + 1, alive()

    lax.while_loop(lambda st: jnp.logical_and(st[0] < n_groups, st[1]), group, (jnp.int32(0), alive()))
    o_ref[...] = acc_ref[...].astype(o_ref.dtype)


def sb_attention(q, kv, q_norm_g, k_norm_g, *, batch, seq, tq=1024, tk=256, unroll=1):
    T, HD = q.shape
    dh = q_norm_g.shape[-1]
    H = HD // dh
    tq = _tile(seq, tq)
    tk = _tile(tq, tk)
    nq = seq // tq
    unroll = math.gcd(unroll, tq // tk)
    qg = q_norm_g.reshape(1, dh).astype(F32)
    kg = k_norm_g.reshape(1, dh).astype(F32)
    est = 4 * seq * dh * 2 + seq * dh * 2 + 16 * tq * tk * 4 + (1 << 22)
    return pl.pallas_call(
        functools.partial(_sb_kernel, tq=tq, tk=tk, unroll=unroll),
        grid=(batch, H, nq),
        in_specs=[pl.BlockSpec((tq, dh), lambda b, h, i: (b * nq + i, h)),
                  pl.BlockSpec((seq, dh), lambda b, h, i: (b, h)),
                  pl.BlockSpec((seq, dh), lambda b, h, i: (b, H + h)),
                  pl.BlockSpec((1, dh), lambda b, h, i: (0, 0)),
                  pl.BlockSpec((1, dh), lambda b, h, i: (0, 0))],
        out_specs=pl.BlockSpec((tq, dh), lambda b, h, i: (b * nq + i, h)),
        out_shape=jax.ShapeDtypeStruct((T, HD), BF16),
        scratch_shapes=[pltpu.VMEM((seq, dh), BF16), pltpu.VMEM((tq, dh), F32),
                        pltpu.VMEM((tq, 1), F32)],
        compiler_params=pltpu.CompilerParams(
            dimension_semantics=("parallel", "parallel", "arbitrary"),
            vmem_limit_bytes=_vmem_limit(est)),
        name="sb_attention",
    )(q, kv, kv, qg, kg)


def kernel(x, attn_norm_g, ret_w_in, ret_head_norm_g, ret_w_out, kv_norm_g, w_kv, k_norm_g, sb_w_q, q_norm_g,
           sb_w_out, ffn_norm_g, peer_w_q, peer_subkeys, peer_u, peer_v):
    B, S, D = x.shape
    T = B * S
    depth = attn_norm_g.shape[0]
    n_a = ret_w_in.shape[0]
    h = x.reshape(T, D)
    u_bf = v_bf = None
    for layer in range(depth):
        own = ((peer_u, layer), (peer_v, layer)) if layer == 0 else ()
        nxt = ((peer_u, layer + 1), (peer_v, layer + 1)) if layer + 1 < depth else ()
        if layer < n_a:
            (xn,) = rms_norms(h, attn_norm_g[layer:layer + 1])
            proj, *cast = matmul(xn, to_bf16(ret_w_in, layer), side=own) if own else (
                matmul(xn, to_bf16(ret_w_in, layer)),)
            u_bf, v_bf = cast or (u_bf, v_bf)
            o = retention(proj, ret_head_norm_g[layer], batch=B, seq=S)
            w_o = to_bf16(ret_w_out, layer)
        else:
            b = layer - n_a
            if layer == n_a:
                kvn, xn = rms_norms(h, jnp.stack([kv_norm_g, attn_norm_g[layer]]))
                kv = matmul(kvn, to_bf16(w_kv))
            else:
                (xn,) = rms_norms(h, attn_norm_g[layer:layer + 1])
            if own:
                q, u_bf, v_bf = matmul(xn, to_bf16(sb_w_q, b), side=own)
            else:
                q = matmul(xn, to_bf16(sb_w_q, b))
            o = sb_attention(q, kv, q_norm_g[b], k_norm_g, batch=B, seq=S)
            w_o = to_bf16(sb_w_out, b)
        if nxt:
            h_att, u_nxt, v_nxt = matmul(o, w_o, residuals=(h,), side=nxt, out_dtype=F32, tk=2048)
        else:
            h_att, u_nxt, v_nxt = matmul(o, w_o, residuals=(h,), out_dtype=F32, tk=2048), None, None
        (fn,) = rms_norms(h_att, ffn_norm_g[layer:layer + 1])
        h = peer_ffn(fn, h_att, to_bf16(peer_w_q, layer), peer_subkeys[layer].astype(BF16), u_bf, v_bf)
        u_bf, v_bf = u_nxt, v_nxt
    return h.reshape(B, S, D)
```

```python
import functools
import math

import jax
import jax.numpy as jnp
from jax import lax
from jax.experimental import pallas as pl
from jax.experimental.pallas import tpu as pltpu

F32 = jnp.float32
BF16 = jnp.bfloat16

NORM_EPS = 1e-6
ROPE_BASE = 10000.0
RET_CHUNK = 256
PEER_TOPK = 16
V7X_VMEM_BYTES = 64 * 1024 * 1024
VMEM_RESERVE_BYTES = 6 * 1024 * 1024
NEG_INF = float("-inf")
INV_SQRT2 = 1.0 / math.sqrt(2.0)
LOG2E = 1.0 / math.log(2.0)
SB_DEAD_LOG2 = -160.0


def _vmem_limit(estimate_bytes):
    return int(min(max(estimate_bytes, 16 * 1024 * 1024), V7X_VMEM_BYTES - VMEM_RESERVE_BYTES))


def _tile(n, pref):
    t = min(n, pref)
    assert n % t == 0, (n, pref)
    return t


def _norm_kernel(x_ref, g_ref, *out_refs):
    x = x_ref[...]
    y = x * lax.rsqrt(jnp.mean(x * x, axis=-1, keepdims=True) + NORM_EPS)
    for gi, o_ref in enumerate(out_refs):
        o_ref[...] = (y * g_ref[gi:gi + 1, :]).astype(o_ref.dtype)


def rms_norms(x, gains, *, tm=256):
    T, D = x.shape
    n = gains.shape[0]
    tm = _tile(T, tm)
    row = pl.BlockSpec((tm, D), lambda i: (i, 0))
    est = 2 * tm * D * (4 + 2 * n) + 4 * tm * D * 4
    return pl.pallas_call(
        _norm_kernel,
        grid=(T // tm,),
        in_specs=[row, pl.BlockSpec((n, D), lambda i: (0, 0))],
        out_specs=[row] * n,
        out_shape=[jax.ShapeDtypeStruct((T, D), BF16)] * n,
        compiler_params=pltpu.CompilerParams(
            dimension_semantics=("parallel",), vmem_limit_bytes=_vmem_limit(est)),
        name="rms_norms",
    )(x, gains)


def _cast_kernel(x_ref, o_ref):
    o_ref[...] = x_ref[...].astype(o_ref.dtype)


def to_bf16(w, index=None, *, tr=512, tc=4096):
    R, C = w.shape[-2:]
    tr, tc = _tile(R, tr), _tile(C, tc)
    if index is None:
        in_spec = pl.BlockSpec((tr, tc), lambda i, j: (i, j))
    else:
        in_spec = pl.BlockSpec((None, tr, tc), lambda i, j: (index, i, j))
    return pl.pallas_call(
        _cast_kernel,
        grid=(R // tr, C // tc),
        in_specs=[in_spec],
        out_specs=pl.BlockSpec((tr, tc), lambda i, j: (i, j)),
        out_shape=jax.ShapeDtypeStruct((R, C), BF16),
        compiler_params=pltpu.CompilerParams(
            dimension_semantics=("parallel", "parallel"), vmem_limit_bytes=_vmem_limit(2 * tr * tc * 6 + (1 << 22))),
        name="to_bf16",
    )(w)


def _mm_kernel(*refs, nk, n_res, n_side):
    a_ref, w_ref = refs[0], refs[1]
    res_refs = refs[2:2 + n_res]
    side_in = refs[2 + n_res:2 + n_res + n_side]
    o_ref = refs[2 + n_res + n_side]
    side_out = refs[3 + n_res + n_side:3 + n_res + 2 * n_side]
    for s_in, s_out in zip(side_in, side_out):
        s_out[...] = s_in[...].astype(s_out.dtype)
    if nk == 1:
        acc = jnp.dot(a_ref[...], w_ref[...], preferred_element_type=F32)
        for r in res_refs:
            acc = acc + r[...]
        o_ref[...] = acc.astype(o_ref.dtype)
        return
    acc_ref = refs[3 + n_res + 2 * n_side]
    k = pl.program_id(2)

    @pl.when(k == 0)
    def _():
        acc_ref[...] = jnp.zeros_like(acc_ref)

    acc_ref[...] += jnp.dot(a_ref[...], w_ref[...], preferred_element_type=F32)

    @pl.when(k == nk - 1)
    def _():
        acc = acc_ref[...]
        for r in res_refs:
            acc = acc + r[...]
        o_ref[...] = acc.astype(o_ref.dtype)


def _side_rows(n_rows, n_steps):
    for rb in range(16, n_rows + 1, 16):
        if n_rows % rb == 0 and rb * n_steps >= n_rows:
            return rb
    return n_rows


def matmul(a, w, *, residuals=(), side=(), out_dtype=BF16, tm=1024, tn=1024, tk=4096):
    M, K = a.shape
    N = w.shape[1]
    tm, tn, tk = _tile(M, tm), _tile(N, tn), _tile(K, tk)
    nk = K // tk
    nj = N // tn
    n_res, n_side = len(residuals), len(side)
    n_steps = (M // tm) * nj * nk
    in_specs = [pl.BlockSpec((tm, tk), lambda i, j, k: (i, k)),
                pl.BlockSpec((tk, tn), lambda i, j, k: (k, j))]
    in_specs += [pl.BlockSpec((tm, tn), lambda i, j, k: (i, j))] * n_res
    out_specs = [pl.BlockSpec((tm, tn), lambda i, j, k: (i, j))]
    out_shape = [jax.ShapeDtypeStruct((M, N), out_dtype)]
    osz = jnp.dtype(out_dtype).itemsize
    est = 2 * (tm * tk * 2 + tk * tn * 2 + tm * tn * (osz + 4 * n_res)) + 2 * tm * tn * 4
    for p, l in side:
        _, R, C = p.shape
        rb = _side_rows(R, n_steps)
        last = R // rb - 1
        blk = lambda i, j, k, last=last: jnp.minimum((i * nj + j) * nk + k, last)
        in_specs.append(pl.BlockSpec((None, rb, C), lambda i, j, k, l=l, blk=blk: (l, blk(i, j, k), 0)))
        out_specs.append(pl.BlockSpec((rb, C), lambda i, j, k, blk=blk: (blk(i, j, k), 0)))
        out_shape.append(jax.ShapeDtypeStruct((R, C), BF16))
        est += 2 * rb * C * 6
    outs = pl.pallas_call(
        functools.partial(_mm_kernel, nk=nk, n_res=n_res, n_side=n_side),
        grid=(M // tm, nj, nk),
        in_specs=in_specs,
        out_specs=out_specs,
        out_shape=out_shape,
        scratch_shapes=[pltpu.VMEM((tm, tn), F32)] if nk > 1 else [],
        compiler_params=pltpu.CompilerParams(
            dimension_semantics=("arbitrary",) * 3 if n_side else ("parallel", "parallel", "arbitrary"),
            vmem_limit_bytes=_vmem_limit(est)),
        name="matmul",
    )(a, w, *residuals, *[p for p, _ in side])
    return outs if n_side else outs[0]


def _ret_kernel(q_ref, k_ref, v_ref, gate_ref, cos_ref, sin_ref, dmat_ref, xi_ref, zeta_ref,
                gch_ref, hg_ref, o_ref, r_ref, *, n_chunks, chunk, dk):
    @pl.when(pl.program_id(2) == 0)
    def _():
        r_ref[...] = jnp.zeros_like(r_ref)

    half = dk // 2
    kscale = dk ** -0.5

    def rot(t, cos, sin):
        t1, t2 = t[:, :half], t[:, half:]
        return jnp.concatenate([t1 * cos - t2 * sin, t1 * sin + t2 * cos], axis=-1)

    def body(c):
        rows = pl.ds(c * chunk, chunk)
        cos, sin = cos_ref[rows, :], sin_ref[rows, :]
        q = rot(q_ref[rows, :].astype(F32), cos, sin)
        k = rot(k_ref[rows, :].astype(F32), cos, sin) * kscale
        vb = v_ref[rows, :]
        inner = lax.dot_general(q.astype(BF16), k.astype(BF16), (((1,), (1,)), ((), ())),
                                preferred_element_type=F32) * dmat_ref[...]
        r_old = r_ref[...]
        o = (jnp.dot(inner.astype(BF16), vb, preferred_element_type=F32)
             + jnp.dot((q * xi_ref[...]).astype(BF16), r_old.astype(BF16),
                       preferred_element_type=F32))
        kz = (k * zeta_ref[...]).astype(BF16)
        r_ref[...] = r_old * gch_ref[...] + lax.dot_general(
            kz, vb, (((0,), (0,)), ((), ())), preferred_element_type=F32)
        on = o * lax.rsqrt(jnp.mean(o * o, axis=-1, keepdims=True) + NORM_EPS) * hg_ref[...]
        g = gate_ref[rows, :].astype(F32)
        silu = g / (1.0 + jnp.exp(-g))
        o_ref[rows, :] = (silu * on).astype(o_ref.dtype)

    for c in range(n_chunks):
        body(c)


def retention(proj, head_norm_g, *, batch, seq, rows_per_step=1024):
    T, _ = proj.shape
    H, dv = head_norm_g.shape
    dk = dv // 2
    C = RET_CHUNK
    sb = _tile(seq, rows_per_step)
    ns = seq // sb

    inv = 1.0 / (ROPE_BASE ** (jnp.arange(0, dk, 2, dtype=F32) / dk))
    ang = jnp.arange(seq, dtype=F32)[:, None] * inv[None, :]
    cos, sin = jnp.cos(ang), jnp.sin(ang)
    log_g = jnp.log1p(-jnp.exp2(-5.0 - jnp.arange(H, dtype=F32)))
    i = jnp.arange(C, dtype=F32)
    diff = i[:, None] - i[None, :]
    lower = diff >= 0
    dmat = jnp.where(lower[None], jnp.exp(jnp.where(lower, diff, 0.0)[None] * log_g[:, None, None]), 0.0)
    xi = jnp.exp((i[None, :] + 1.0) * log_g[:, None])[:, :, None]
    zeta = jnp.exp((C - 1.0 - i[None, :]) * log_g[:, None])[:, :, None]
    gch = jnp.broadcast_to(jnp.exp(C * log_g)[:, None, None], (H, 1, dv))
    hg = head_norm_g.reshape(H, 1, dv).astype(F32)

    nqk = H * dk // dk
    row_idx = lambda b, h, s: b * ns + s
    in_specs = [
        pl.BlockSpec((sb, dk), lambda b, h, s: (row_idx(b, h, s), h)),
        pl.BlockSpec((sb, dk), lambda b, h, s: (row_idx(b, h, s), nqk + h)),
        pl.BlockSpec((sb, dv), lambda b, h, s: (row_idx(b, h, s), (2 * H * dk) // dv + h)),
        pl.BlockSpec((sb, dv), lambda b, h, s: (row_idx(b, h, s), (2 * H * dk) // dv + H + h)),
        pl.BlockSpec((sb, dk // 2), lambda b, h, s: (s, 0)),
        pl.BlockSpec((sb, dk // 2), lambda b, h, s: (s, 0)),
        pl.BlockSpec((None, C, C), lambda b, h, s: (h, 0, 0)),
        pl.BlockSpec((None, C, 1), lambda b, h, s: (h, 0, 0)),
        pl.BlockSpec((None, C, 1), lambda b, h, s: (h, 0, 0)),
        pl.BlockSpec((None, 1, dv), lambda b, h, s: (h, 0, 0)),
        pl.BlockSpec((None, 1, dv), lambda b, h, s: (h, 0, 0)),
    ]
    est = 2 * sb * (2 * dk * 2 + 3 * dv * 2 + dk * 4) + 8 * dk * dv * 4 + (1 << 22)
    return pl.pallas_call(
        functools.partial(_ret_kernel, n_chunks=sb // C, chunk=C, dk=dk),
        grid=(batch, H, ns),
        in_specs=in_specs,
        out_specs=pl.BlockSpec((sb, dv), lambda b, h, s: (row_idx(b, h, s), h)),
        out_shape=jax.ShapeDtypeStruct((T, H * dv), BF16),
        scratch_shapes=[pltpu.VMEM((dk, dv), F32)],
        compiler_params=pltpu.CompilerParams(
            dimension_semantics=("parallel", "parallel", "arbitrary"),
            vmem_limit_bytes=_vmem_limit(est)),
        name="retention",
    )(proj, proj, proj, proj, cos, sin, dmat, xi, zeta, gch, hg)


def _peer_pairs(topk):
    return [(r1, r2) for r1 in range(topk) for r2 in range(topk) if (r1 + 1) * (r2 + 1) <= topk]


def _first_max(work, iota, n):
    m = jnp.max(work, axis=0, keepdims=True)
    idx = jnp.min(jnp.where(work == m, iota, n), axis=0, keepdims=True)
    return m, iota == idx


def _route_kernel(q_ref, sk_ref, n1_ref, p1_ref, r2_ref, p2_ref, s_ref, vals_ref, cand_ref, *, topk, lane_chunk):
    n_keys, half = sk_ref.shape[1], sk_ref.shape[2]
    tm = q_ref.shape[0]
    pairs = _peer_pairs(topk)
    n_cand = cand_ref.shape[0]
    iota_k = lax.broadcasted_iota(jnp.int32, (n_keys, lane_chunk), 0)
    iota_c = lax.broadcasted_iota(jnp.int32, (n_cand, lane_chunk), 0)
    for side in range(2):
        s_ref[side] = lax.dot_general(sk_ref[side], q_ref[:, side * half:(side + 1) * half],
                                      (((1,), (1,)), ((), ())), preferred_element_type=F32)
    cand_ref[...] = jnp.full(cand_ref.shape, NEG_INF, F32)
    for lc in range(tm // lane_chunk):
        ls = slice(lc * lane_chunk, (lc + 1) * lane_chunk)
        ranks = []
        for side in range(2):
            work = s_ref[side, :, ls]
            rank = jnp.full((n_keys, lane_chunk), float(topk), F32)
            for r in range(topk):
                m, pick = _first_max(work, iota_k, n_keys)
                vals_ref[side, r:r + 1, ls] = m
                work = jnp.where(pick, NEG_INF, work)
                rank = jnp.where(pick, float(r), rank)
            ranks.append(rank)
        for p, (r1, r2) in enumerate(pairs):
            cand_ref[p:p + 1, ls] = vals_ref[0, r1:r1 + 1, ls] + vals_ref[1, r2:r2 + 1, ls]
        work = cand_ref[:, ls]
        cmax = None
        z = jnp.zeros((1, lane_chunk), F32)
        for r in range(topk):
            m, pick = _first_max(work, iota_c, n_cand)
            if r == 0:
                cmax = m
            z = z + jnp.exp(m - cmax)
            work = jnp.where(pick, NEG_INF, work)
        taken = jnp.where((work == NEG_INF) & (iota_c < len(pairs)), 1.0, 0.0)
        n1 = jnp.zeros((n_keys, lane_chunk), F32)
        start = 0
        for r1 in range(topk):
            cnt = topk // (r1 + 1)
            n_r1 = jnp.sum(taken[start:start + cnt, :], axis=0, keepdims=True)
            n1 = jnp.where(ranks[0] == float(r1), n_r1, n1)
            start += cnt
        inv_z = 1.0 / z
        n1_ref[:, ls] = n1
        p1_ref[:, ls] = jnp.where(ranks[0] < float(topk),
                                  jnp.exp(s_ref[0, :, ls] - vals_ref[0, 0:1, ls]) * inv_z, 0.0)
        r2_ref[:, ls] = ranks[1].astype(BF16)
        p2_ref[:, ls] = jnp.where(ranks[1] < float(topk),
                                  jnp.exp(s_ref[1, :, ls] - vals_ref[1, 0:1, ls]), 0.0).astype(BF16)


def peer_route(q, subkeys, *, tm=512, lane_chunk=256):
    T = q.shape[0]
    _, n_keys, half = subkeys.shape
    n_heads = q.shape[1] // (2 * half)
    tm = _tile(T, tm)
    lane_chunk = _tile(tm, lane_chunk)
    n_cand = -(-len(_peer_pairs(PEER_TOPK)) // 8) * 8
    tab = pl.BlockSpec((None, n_keys, tm), lambda i, h: (h, 0, i))
    tab_shape = jax.ShapeDtypeStruct((n_heads, n_keys, T), F32)
    tab_bf = jax.ShapeDtypeStruct((n_heads, n_keys, T), BF16)
    est = 2 * (tm * 2 * half * 2 + 4 * n_keys * tm * 4) + (2 * n_keys + 2 * PEER_TOPK + n_cand) * tm * 4 + (1 << 23)
    return pl.pallas_call(
        functools.partial(_route_kernel, topk=PEER_TOPK, lane_chunk=lane_chunk),
        grid=(T // tm, n_heads),
        in_specs=[pl.BlockSpec((tm, 2 * half), lambda i, h: (i, h)),
                  pl.BlockSpec(subkeys.shape, lambda i, h: (0, 0, 0))],
        out_specs=[tab] * 4,
        out_shape=[tab_shape, tab_shape, tab_bf, tab_bf],
        scratch_shapes=[pltpu.VMEM((2, n_keys, tm), F32), pltpu.VMEM((2, PEER_TOPK, tm), F32),
                        pltpu.VMEM((n_cand, tm), F32)],
        compiler_params=pltpu.CompilerParams(
            dimension_semantics=("parallel", "parallel"), vmem_limit_bytes=_vmem_limit(est)),
        name="peer_route",
    )(q, subkeys)


def _expert_kernel(xn_ref, res_ref, u_ref, v_ref, n1_ref, p1_ref, r2_ref, p2_ref, o_ref, w_ref, *, n_keys, n_i):
    j = pl.program_id(1)
    tm = xn_ref.shape[0]
    n_heads = n1_ref.shape[0]
    sub = 16

    @pl.when(j == 0)
    def _():
        o_ref[...] = res_ref[...]

    def gates(il):
        i_row = j * n_i + il
        gate = jnp.zeros((n_keys // sub, sub, tm), BF16)
        for h in range(n_heads):
            n1b = jnp.broadcast_to(n1_ref[h, pl.ds(i_row, 1), :], (sub, tm)).astype(BF16)
            p1b = jnp.broadcast_to(p1_ref[h, pl.ds(i_row, 1), :], (sub, tm)).astype(BF16)
            r2 = r2_ref[h].reshape(n_keys // sub, sub, tm)
            p2 = p2_ref[h].reshape(n_keys // sub, sub, tm)
            gate = gate + jnp.where(r2 < n1b[None], p2 * p1b[None], jnp.zeros_like(p2))
        return gate.reshape(n_keys, tm)

    act = lax.dot_general(u_ref[...], xn_ref[...], (((1,), (1,)), ((), ())), preferred_element_type=F32)
    for il in range(n_i):
        rows_e = slice(il * n_keys, (il + 1) * n_keys)
        a = act[rows_e, :]
        gelu = 0.5 * a * (1.0 + lax.erf(a * INV_SQRT2))
        w_ref[rows_e, :] = gates(il) * gelu.astype(BF16)
    o_ref[...] += lax.dot_general(w_ref[...], v_ref[...], (((0,), (0,)), ((), ())), preferred_element_type=F32)


def peer_experts(xn, res, u, v, n1, p1, r2, p2, *, tm=512, te=512):
    T, D = xn.shape
    E = u.shape[0]
    n_heads, n_keys, _ = n1.shape
    tm, te = _tile(T, tm), _tile(E, te)
    assert te % n_keys == 0
    once = dict(pipeline_mode=pl.Buffered(1))
    tab = pl.BlockSpec((n_heads, n_keys, tm), lambda i, j: (0, 0, i), **once)
    est = (tm * D * 2 + tm * D * 4 + 2 * 2 * te * D * 2 + 3 * n_heads * n_keys * tm * 4 + 2 * tm * D * 4
           + te * tm * 8 + tm * D * 4 + (1 << 22))
    return pl.pallas_call(
        functools.partial(_expert_kernel, n_keys=n_keys, n_i=te // n_keys),
        grid=(T // tm, E // te),
        in_specs=[pl.BlockSpec((tm, D), lambda i, j: (i, 0), **once),
                  pl.BlockSpec((tm, D), lambda i, j: (i, 0), **once),
                  pl.BlockSpec((te, D), lambda i, j: (j, 0)),
                  pl.BlockSpec((te, D), lambda i, j: (j, 0)),
                  tab, tab, tab, tab],
        out_specs=pl.BlockSpec((tm, D), lambda i, j: (i, 0)),
        out_shape=jax.ShapeDtypeStruct((T, D), F32),
        scratch_shapes=[pltpu.VMEM((te, tm), BF16)],
        compiler_params=pltpu.CompilerParams(
            dimension_semantics=("parallel", "arbitrary"), vmem_limit_bytes=_vmem_limit(est)),
        name="peer_experts",
    )(xn, res, u, v, n1, p1, r2, p2)


def peer_ffn(xn, res, w_q, subkeys, u, v):
    q = matmul(xn, w_q)
    return peer_experts(xn, res, u, v, *peer_route(q, subkeys))


def _head_norm(t, g, scale):
    t = t.astype(F32)
    return t * lax.rsqrt(jnp.mean(t * t, axis=-1, keepdims=True) + NORM_EPS) * (g * scale)


def _sb_kernel(q_ref, k_ref, v_ref, qg_ref, kg_ref, o_ref, kn_ref, acc_ref, carry_ref, *, tq, tk, unroll):
    qi = pl.program_id(2)
    dh = q_ref.shape[1]
    seq = k_ref.shape[0]

    @pl.when(qi == 0)
    def _():
        def norm_rows(c, carry):
            rows = pl.ds(pl.multiple_of(c * tq, tq), tq)
            kn_ref[rows, :] = _head_norm(k_ref[rows, :], kg_ref[...], 1.0).astype(BF16)
            return carry
        lax.fori_loop(0, seq // tq, norm_rows, 0)

    qn = _head_norm(q_ref[...], qg_ref[...], dh ** -0.5 * LOG2E).astype(BF16)
    acc_ref[...] = jnp.zeros_like(acc_ref)
    carry_ref[...] = jnp.zeros_like(carry_ref)
    jj = lax.broadcasted_iota(jnp.int32, (tk, tk), 0)
    ss = lax.broadcasted_iota(jnp.int32, (tk, tk), 1)
    upper = (jj > ss).astype(BF16)

    def scores(kb, row0):
        rows = pl.ds(pl.multiple_of(kb * tk, tk), tk)
        z = lax.dot_general(qn[row0:, :], kn_ref[rows, :], (((1,), (1,)), ((), ())),
                            preferred_element_type=F32)
        nz = -z
        l1m = jnp.minimum(nz, 0.0) - jnp.log2(1.0 + jnp.exp2(jnp.minimum(z, nz)))
        causal = None
        if row0 is not None:
            t_idx = qi * tq + row0 + lax.broadcasted_iota(jnp.int32, (tq - row0, tk), 0)
            s_idx = kb * tk + lax.broadcasted_iota(jnp.int32, (tq - row0, tk), 1)
            causal = s_idx < t_idx
            l1m = jnp.where(causal, l1m, 0.0)
        after = jnp.dot(l1m.astype(BF16), upper, preferred_element_type=F32)
        return rows, z + l1m + after, after[:, 0:1] + l1m[:, 0:1], causal, row0 or 0

    def accumulate(rows, logit, total, causal, row0):
        carry = carry_ref[row0:, :]
        a = jnp.exp2(logit + carry)
        if causal is not None:
            a = jnp.where(causal, a, 0.0)
        acc_ref[row0:, :] += jnp.dot(a.astype(BF16), v_ref[rows, :], preferred_element_type=F32)
        carry_ref[row0:, :] = carry + total

    n_diag = tq // tk
    for t in [scores(qi * n_diag + c, c * tk) for c in range(n_diag - 1, -1, -1)]:
        accumulate(*t)

    n_groups = (qi * n_diag) // unroll

    def alive():
        return jnp.max(carry_ref[...]) > SB_DEAD_LOG2

    def group(state):
        it, _ = state
        first = qi * n_diag - 1 - it * unroll
        terms = [scores(first - u, None) for u in range(unroll)]
        for t in terms:
            accumulate(*t)
        return it + 1, alive()

    lax.while_loop(lambda st: jnp.logical_and(st[0] < n_groups, st[1]), group, (jnp.int32(0), alive()))
    o_ref[...] = acc_ref[...].astype(o_ref.dtype)


def sb_attention(q, kv, q_norm_g, k_norm_g, *, batch, seq, tq=1024, tk=256, unroll=1):
    T, HD = q.shape
    dh = q_norm_g.shape[-1]
    H = HD // dh
    tq = _tile(seq, tq)
    tk = _tile(tq, tk)
    nq = seq // tq
    unroll = math.gcd(unroll, tq // tk)
    qg = q_norm_g.reshape(1, dh).astype(F32)
    kg = k_norm_g.reshape(1, dh).astype(F32)
    est = 4 * seq * dh * 2 + seq * dh * 2 + 16 * tq * tk * 4 + (1 << 22)
    return pl.pallas_call(
        functools.partial(_sb_kernel, tq=tq, tk=tk, unroll=unroll),
        grid=(batch, H, nq),
        in_specs=[pl.BlockSpec((tq, dh), lambda b, h, i: (b * nq + i, h)),
                  pl.BlockSpec((seq, dh), lambda b, h, i: (b, h)),
                  pl.BlockSpec((seq, dh), lambda b, h, i: (b, H + h)),
                  pl.BlockSpec((1, dh), lambda b, h, i: (0, 0)),
                  pl.BlockSpec((1, dh), lambda b, h, i: (0, 0))],
        out_specs=pl.BlockSpec((tq, dh), lambda b, h, i: (b * nq + i, h)),
        out_shape=jax.ShapeDtypeStruct((T, HD), BF16),
        scratch_shapes=[pltpu.VMEM((seq, dh), BF16), pltpu.VMEM((tq, dh), F32),
                        pltpu.VMEM((tq, 1), F32)],
        compiler_params=pltpu.CompilerParams(
            dimension_semantics=("parallel", "parallel", "arbitrary"),
            vmem_limit_bytes=_vmem_limit(est)),
        name="sb_attention",
    )(q, kv, kv, qg, kg)


def kernel(x, attn_norm_g, ret_w_in, ret_head_norm_g, ret_w_out, kv_norm_g, w_kv, k_norm_g, sb_w_q, q_norm_g,
           sb_w_out, ffn_norm_g, peer_w_q, peer_subkeys, peer_u, peer_v):
    B, S, D = x.shape
    T = B * S
    depth = attn_norm_g.shape[0]
    n_a = ret_w_in.shape[0]
    h = x.reshape(T, D)
    n_b = sb_w_q.shape[0]
    later = {("peer_u", 0): (peer_u, 0), ("peer_v", 0): (peer_v, 0), ("w_kv", 0): (w_kv[None], 0)}
    later.update({("peer_w_q", l): (peer_w_q, l) for l in range(depth)})
    later.update({("ret_w_out", a): (ret_w_out, a) for a in range(n_a)})
    later.update({("ret_w_in", a): (ret_w_in, a) for a in range(1, n_a)})
    later.update({("sb_w_q", b): (sb_w_q, b) for b in range(n_b)})
    later.update({("sb_w_out", b): (sb_w_out, b) for b in range(n_b)})
    hosted = {}

    def bf(name, l=0):
        if (name, l) in hosted:
            return hosted[(name, l)]
        p, idx = later[(name, l)] if (name, l) in later else (dict(ret_w_in=ret_w_in, peer_u=peer_u, peer_v=peer_v)[name], l)
        return to_bf16(p, idx)

    for layer in range(depth):
        nxt = {("peer_u", layer + 1): (peer_u, layer + 1), ("peer_v", layer + 1): (peer_v, layer + 1)} \
            if layer + 1 < depth else {}
        if layer < n_a:
            (xn,) = rms_norms(h, attn_norm_g[layer:layer + 1])
            if layer == 0:
                proj, *cast = matmul(xn, bf("ret_w_in", 0), side=tuple(later.values()))
                hosted.update(zip(later.keys(), cast))
            else:
                proj = matmul(xn, bf("ret_w_in", layer))
            o = retention(proj, ret_head_norm_g[layer], batch=B, seq=S)
            w_o = bf("ret_w_out", layer)
        else:
            b = layer - n_a
            if layer == n_a:
                kvn, xn = rms_norms(h, jnp.stack([kv_norm_g, attn_norm_g[layer]]))
                kv = matmul(kvn, bf("w_kv"))
            else:
                (xn,) = rms_norms(h, attn_norm_g[layer:layer + 1])
            q = matmul(xn, bf("sb_w_q", b))
            o = sb_attention(q, kv, q_norm_g[b], k_norm_g, batch=B, seq=S)
            w_o = bf("sb_w_out", b)
        if nxt:
            h_att, *cast = matmul(o, w_o, residuals=(h,), side=tuple(nxt.values()), out_dtype=F32, tk=2048)
            hosted.update(zip(nxt.keys(), cast))
        else:
            h_att = matmul(o, w_o, residuals=(h,), out_dtype=F32, tk=2048)
        (fn,) = rms_norms(h_att, ffn_norm_g[layer:layer + 1])
        h = peer_ffn(fn, h_att, bf("peer_w_q", layer), peer_subkeys[layer].astype(BF16),
                     bf("peer_u", layer), bf("peer_v", layer))
    return h.reshape(B, S, D)
```
